```python
import math
import jax, jax.numpy as jnp
from jax import lax
import numpy as np

D_MODEL = 1024
BATCH = 8
SEQ = 4096
DEPTH = 2

HEAD_DIM = 64
MOBA_HEADS = (D_MODEL // 2) // HEAD_DIM
MOBA_BLOCK = 256
MOBA_TOPK = 3
MOBA_QCHUNK = 32
DIFF_QK_DIM = HEAD_DIM
DIFF_V_DIM = 2 * DIFF_QK_DIM
DIFF_HEADS = (D_MODEL // 2) // DIFF_V_DIM
DIFF_QBLOCK = 128
D_FF = ((8 * D_MODEL // 3 + 127) // 128) * 128
RMS_EPS = 1e-6
MOBA_W = MOBA_HEADS * HEAD_DIM
DIFF_QK_W = DIFF_HEADS * 2 * DIFF_QK_DIM
DIFF_V_W = DIFF_HEADS * DIFF_V_DIM
MIX_W = MOBA_W + DIFF_V_W
IN_W = 3 * MOBA_W + 2 * DIFF_QK_W + DIFF_V_W
SPLITS = tuple(np.cumsum([MOBA_W, MOBA_W, MOBA_W, DIFF_QK_W, DIFF_QK_W]).tolist())

kernel_name = "hymba_moba_diffattn_macaron_sandwich"


def alibi_slopes(n):
    return jnp.asarray(2.0 ** (-8.0 * np.arange(1, n + 1) / n), dtype=jnp.float32)


def rmsnorm(x, g):
    xf = x.astype(jnp.float32)
    y = xf * lax.rsqrt(jnp.mean(xf * xf, axis=-1, keepdims=True) + RMS_EPS)
    return (y * g.astype(jnp.float32)).astype(x.dtype)


def swiglu(h, w_gate, w_up, w_down):
    return (jax.nn.silu(h @ w_gate) * (h @ w_up)) @ w_down


def moba_attention(q, k, v, slopes):
    B, H, S, dh = q.shape
    nb = -(-S // MOBA_BLOCK)
    sp = nb * MOBA_BLOCK
    pad = sp - S
    if pad:
        q, k, v = [jnp.pad(t, ((0, 0), (0, 0), (0, pad), (0, 0))) for t in (q, k, v)]
    kb = k.reshape(B, H, nb, MOBA_BLOCK, dh)
    vb = v.reshape(B, H, nb, MOBA_BLOCK, dh)
    kmean = jnp.mean(kb.astype(jnp.float32), axis=3)
    ksel = min(MOBA_TOPK, nb)
    scale = dh ** -0.5
    nch = sp // MOBA_QCHUNK
    qc = q.reshape(B, H, nch, MOBA_QCHUNK, dh).transpose(2, 0, 1, 3, 4)
    gather = jax.vmap(jax.vmap(lambda blocks, idx: blocks[idx]))
    offs = jnp.arange(MOBA_BLOCK)
    blk_ids = jnp.arange(nb)
    sel_ids = jnp.arange(ksel)

    def chunk(args):
        ci, qq = args
        qpos = ci * MOBA_QCHUNK + jnp.arange(MOBA_QCHUNK)
        cur = (ci * MOBA_QCHUNK) // MOBA_BLOCK
        qf = qq.astype(jnp.float32)
        gate = jnp.einsum('bhqd,bhnd->bhqn', qf, kmean)
        gate = jnp.where(blk_ids < cur, gate, -jnp.inf)
        _, idx = lax.top_k(gate, ksel)
        valid = sel_ids < cur
        k_sel = gather(kb, idx).astype(jnp.float32)
        v_sel = gather(vb, idx).astype(jnp.float32)
        s_sel = jnp.einsum('bhqd,bhqjkd->bhqjk', qf, k_sel) * scale
        kpos_sel = idx[..., None] * MOBA_BLOCK + offs
        s_sel = s_sel - slopes[:, None, None, None] * (qpos[:, None, None] - kpos_sel)
        s_sel = jnp.where(valid[:, None], s_sel, -jnp.inf)
        s_sel = s_sel.reshape(B, H, MOBA_QCHUNK, ksel * MOBA_BLOCK)
        k_own = lax.dynamic_slice_in_dim(kb, cur, 1, axis=2)[:, :, 0].astype(jnp.float32)
        v_own = lax.dynamic_slice_in_dim(vb, cur, 1, axis=2)[:, :, 0].astype(jnp.float32)
        kpos_own = cur * MOBA_BLOCK + offs
        dist = qpos[:, None] - kpos_own[None, :]
        s_own = jnp.einsum('bhqd,bhkd->bhqk', qf, k_own) * scale - slopes[:, None, None] * dist
        s_own = jnp.where(dist >= 0, s_own, -jnp.inf)
        p = jax.nn.softmax(jnp.concatenate([s_sel, s_own], axis=-1), axis=-1)
        p_sel = p[..., :ksel * MOBA_BLOCK].reshape(B, H, MOBA_QCHUNK, ksel, MOBA_BLOCK)
        p_own = p[..., ksel * MOBA_BLOCK:]
        o = (jnp.einsum('bhqjk,bhqjkd->bhqd', p_sel, v_sel)
             + jnp.einsum('bhqk,bhkd->bhqd', p_own, v_own))
        return o.astype(q.dtype)

    out = lax.map(chunk, (jnp.arange(nch), qc))
    out = out.transpose(1, 2, 0, 3, 4).reshape(B, H, sp, dh)
    return out[:, :, :S]


def diff_attention(q, k, v, lam, lam_init, subln_g, slopes):
    B, H, _, S, d = q.shape
    nqb = S // DIFF_QBLOCK
    scale = d ** -0.5
    qb = q.reshape(B, H, 2, nqb, DIFF_QBLOCK, d).transpose(3, 0, 1, 2, 4, 5)
    kf = k.astype(jnp.float32)
    vf = v.astype(jnp.float32)
    kpos = jnp.arange(S)

    def blk(args):
        bi, qq = args
        qpos = bi * DIFF_QBLOCK + jnp.arange(DIFF_QBLOCK)
        dist = qpos[:, None] - kpos[None, :]
        s = jnp.einsum('bhmqd,bhmkd->bhmqk', qq.astype(jnp.float32), kf) * scale
        s = s - slopes[:, None, None, None] * dist
        s = jnp.where(dist >= 0, s, -jnp.inf)
        p = jax.nn.softmax(s, axis=-1)
        a = p[:, :, 0] - lam * p[:, :, 1]
        return jnp.einsum('bhqk,bhkd->bhqd', a, vf)

    out = lax.map(blk, (jnp.arange(nqb), qb))
    out = out.transpose(1, 2, 0, 3, 4).reshape(B, H, S, 2 * d)
    out = rmsnorm(out, subln_g) * (1.0 - lam_init)
    return out.astype(q.dtype)


def setup_inputs(seed: int = 0) -> dict:
    key = jax.random.key(seed)
    ks = jax.random.split(key, 24)
    L = DEPTH

    def w(k, shape, fan_in):
        return jax.random.normal(k, shape, jnp.float32) * (fan_in ** -0.5)

    def gain(k, shape):
        return 1.0 + 0.02 * jax.random.normal(k, shape, jnp.float32)

    return {
        "x": jax.random.normal(ks[0], (BATCH, SEQ, D_MODEL), jnp.float32),
        "ffn1_pre_g": gain(ks[1], (L, D_MODEL)),
        "ffn1_w_gate": w(ks[2], (L, D_MODEL, D_FF), D_MODEL),
        "ffn1_w_up": w(ks[3], (L, D_MODEL, D_FF), D_MODEL),
        "ffn1_w_down": w(ks[4], (L, D_FF, D_MODEL), D_FF),
        "ffn1_post_g": gain(ks[5], (L, D_MODEL)),
        "mix_pre_g": gain(ks[6], (L, D_MODEL)),
        "w_in": w(ks[7], (L, D_MODEL, IN_W), D_MODEL),
        "lambda_q1": 0.1 * jax.random.normal(ks[8], (L, DIFF_QK_DIM), jnp.float32),
        "lambda_k1": 0.1 * jax.random.normal(ks[9], (L, DIFF_QK_DIM), jnp.float32),
        "lambda_q2": 0.1 * jax.random.normal(ks[10], (L, DIFF_QK_DIM), jnp.float32),
        "lambda_k2": 0.1 * jax.random.normal(ks[11], (L, DIFF_QK_DIM), jnp.float32),
        "subln_g": gain(ks[12], (L, DIFF_V_DIM)),
        "w_out": w(ks[13], (L, MIX_W, D_MODEL), MIX_W),
        "mix_post_g": gain(ks[14], (L, D_MODEL)),
        "ffn2_pre_g": gain(ks[15], (L, D_MODEL)),
        "ffn2_w_gate": w(ks[16], (L, D_MODEL, D_FF), D_MODEL),
        "ffn2_w_up": w(ks[17], (L, D_MODEL, D_FF), D_MODEL),
        "ffn2_w_down": w(ks[18], (L, D_FF, D_MODEL), D_FF),
        "ffn2_post_g": gain(ks[19], (L, D_MODEL)),
    }


def reference(x, ffn1_pre_g, ffn1_w_gate, ffn1_w_up, ffn1_w_down, ffn1_post_g,
              mix_pre_g, w_in, lambda_q1, lambda_k1, lambda_q2, lambda_k2, subln_g,
              w_out, mix_post_g, ffn2_pre_g, ffn2_w_gate, ffn2_w_up, ffn2_w_down, ffn2_post_g):
    B, S, _ = x.shape
    slopes_moba = alibi_slopes(MOBA_HEADS)
    slopes_diff = alibi_slopes(DIFF_HEADS)
    for l in range(DEPTH):
        h = rmsnorm(x, ffn1_pre_g[l])
        x = x + 0.5 * rmsnorm(swiglu(h, ffn1_w_gate[l], ffn1_w_up[l], ffn1_w_down[l]), ffn1_post_g[l])

        h = rmsnorm(x, mix_pre_g[l])
        proj = h @ w_in[l]
        q_m, k_m, v_m, q_d, k_d, v_d = jnp.split(proj, SPLITS, axis=-1)
        to_heads = lambda t: t.reshape(B, S, MOBA_HEADS, HEAD_DIM).transpose(0, 2, 1, 3)
        o_m = moba_attention(to_heads(q_m), to_heads(k_m), to_heads(v_m), slopes_moba)
        o_m = o_m.transpose(0, 2, 1, 3).reshape(B, S, MOBA_W)

        to_pairs = lambda t: t.reshape(B, S, DIFF_HEADS, 2, DIFF_QK_DIM).transpose(0, 2, 3, 1, 4)
        v_dh = v_d.reshape(B, S, DIFF_HEADS, DIFF_V_DIM).transpose(0, 2, 1, 3)
        lam_init = 0.8 - 0.6 * math.exp(-0.3 * l)
        lam = (jnp.exp(jnp.sum(lambda_q1[l].astype(jnp.float32) * lambda_k1[l].astype(jnp.float32)))
               - jnp.exp(jnp.sum(lambda_q2[l].astype(jnp.float32) * lambda_k2[l].astype(jnp.float32)))
               + lam_init)
        o_d = diff_attention(to_pairs(q_d), to_pairs(k_d), v_dh, lam, lam_init, subln_g[l], slopes_diff)
        o_d = o_d.transpose(0, 2, 1, 3).reshape(B, S, DIFF_V_W)

        mix = jnp.concatenate([o_m, o_d], axis=-1)
        x = x + rmsnorm(mix @ w_out[l], mix_post_g[l])

        h = rmsnorm(x, ffn2_pre_g[l])
        x = x + 0.5 * rmsnorm(swiglu(h, ffn2_w_gate[l], ffn2_w_up[l], ffn2_w_down[l]), ffn2_post_g[l])
    return x
```

```python
import functools
import math

import jax
import jax.numpy as jnp
import numpy as np
from jax import lax
from jax.experimental import pallas as pl
from jax.experimental.pallas import tpu as pltpu

F32 = jnp.float32
BF16 = jnp.bfloat16

D_MODEL = 1024
HEAD_DIM = 64
MOBA_HEADS = 8
MOBA_BLOCK = 256
MOBA_TOPK = 3
DIFF_HEADS = 4
DIFF_V_DIM = 128
MOBA_W = 512
D_FF = 2816
RMS_EPS = 1e-6

FF_CHUNK = 256
N_FF_CHUNKS = D_FF // FF_CHUNK
TOKEN_TILE = 512
ATT_BLOCK = 256
PAIR_W = 128
VMEM_LIMIT = 56 * 1024 * 1024

NEG_INF = float("-inf")


def _rms(x):
    return x * lax.rsqrt(jnp.mean(x * x, axis=-1, keepdims=True) + RMS_EPS)


def _ffn_kernel(x_ref, gpre_ref, wgu_ref, wd_ref, gpost_ref, o_ref, h_ref, acc_ref):
    x = x_ref[...]
    h_ref[...] = (_rms(x) * gpre_ref[...]).astype(BF16)

    def body(c, carry):
        gu = jnp.dot(h_ref[...], wgu_ref[c], preferred_element_type=F32)
        g = gu[:, :FF_CHUNK]
        u = gu[:, FF_CHUNK:]
        a = (g * jax.nn.sigmoid(g) * u).astype(BF16)
        d = jnp.dot(a, wd_ref[c], preferred_element_type=F32)

        @pl.when(c == 0)
        def _():
            acc_ref[...] = d

        @pl.when(c > 0)
        def _():
            acc_ref[...] += d

        return carry

    lax.fori_loop(0, N_FF_CHUNKS, body, 0)
    y = acc_ref[...]
    o_ref[...] = x_ref[...] + 0.5 * (_rms(y) * gpost_ref[...])


def _ffn(x2, gpre, wgu, wd, gpost):
    T = x2.shape[0]
    const = lambda *shape: pl.BlockSpec(shape, lambda i: (0,) * len(shape), pipeline_mode=pl.Buffered(1))
    return pl.pallas_call(
        _ffn_kernel,
        grid=(T // TOKEN_TILE,),
        in_specs=[
            pl.BlockSpec((TOKEN_TILE, D_MODEL), lambda i: (i, 0)),
            const(1, D_MODEL),
            const(N_FF_CHUNKS, D_MODEL, 2 * FF_CHUNK),
            const(N_FF_CHUNKS, FF_CHUNK, D_MODEL),
            const(1, D_MODEL),
        ],
        out_specs=pl.BlockSpec((TOKEN_TILE, D_MODEL), lambda i: (i, 0)),
        out_shape=jax.ShapeDtypeStruct((T, D_MODEL), F32),
        scratch_shapes=[pltpu.VMEM((TOKEN_TILE, D_MODEL), BF16), pltpu.VMEM((TOKEN_TILE, D_MODEL), F32)],
        compiler_params=pltpu.CompilerParams(dimension_semantics=("parallel",), vmem_limit_bytes=VMEM_LIMIT),
        name="ffn",
    )(x2, gpre, wgu, wd, gpost)


def _proj_kernel(x_ref, g_ref, wqT_ref, wk_ref, wvT_ref, qT_ref, k_ref, vT_ref):
    h = (_rms(x_ref[0]) * g_ref[...]).astype(BF16)
    nt = (((1,), (1,)), ((), ()))
    kk = jnp.dot(h, wk_ref[...], preferred_element_type=F32).astype(BF16)
    qT = lax.dot_general(wqT_ref[...], h, nt, preferred_element_type=F32).astype(BF16)
    vT = lax.dot_general(wvT_ref[...], h, nt, preferred_element_type=F32).astype(BF16)
    for j in range(TOKEN_TILE // ATT_BLOCK):
        sl = slice(j * ATT_BLOCK, (j + 1) * ATT_BLOCK)
        k_ref[0, j] = kk[sl, :]
        qT_ref[0, j] = qT[:, sl]
        vT_ref[0, j] = vT[:, sl]


def _proj(x, g, wqT, wk, wvT):
    B, S, _ = x.shape
    nblk = S // ATT_BLOCK
    per = TOKEN_TILE // ATT_BLOCK
    const = lambda *shape: pl.BlockSpec(shape, lambda b, i: (0,) * len(shape), pipeline_mode=pl.Buffered(1))
    t_spec = pl.BlockSpec((1, per, D_MODEL, ATT_BLOCK), lambda b, i: (b, i, 0, 0))
    n_spec = pl.BlockSpec((1, per, ATT_BLOCK, D_MODEL), lambda b, i: (b, i, 0, 0))
    t_shape = jax.ShapeDtypeStruct((B, nblk, D_MODEL, ATT_BLOCK), BF16)
    n_shape = jax.ShapeDtypeStruct((B, nblk, ATT_BLOCK, D_MODEL), BF16)
    return pl.pallas_call(
        _proj_kernel,
        grid=(B, S // TOKEN_TILE),
        in_specs=[
            pl.BlockSpec((1, TOKEN_TILE, D_MODEL), lambda b, i: (b, i, 0)),
            const(1, D_MODEL),
            const(D_MODEL, D_MODEL),
            const(D_MODEL, D_MODEL),
            const(D_MODEL, D_MODEL),
        ],
        out_specs=[t_spec, n_spec, t_spec],
        out_shape=[t_shape, n_shape, t_shape],
        compiler_params=pltpu.CompilerParams(dimension_semantics=("parallel", "parallel"),
                                             vmem_limit_bytes=VMEM_LIMIT),
        name="mix_proj",
    )(x, g, wqT, wk, wvT)


def _moba_kernel(slopes_ref, qT_ref, k_ref, vT_ref, o_ref, kmean_ref, sel_ref, *, nblk):
    p = pl.program_id(1)
    i = pl.program_id(2)

    @pl.when(i == 0)
    def _():
        for n in range(nblk):
            kmean_ref[n:n + 1, :] = jnp.mean(k_ref[0, n].astype(F32), axis=0, keepdims=True)

    qT = qT_ref[0, 0]
    frow = lax.broadcasted_iota(jnp.int32, (PAIR_W, ATT_BLOCK), 0)
    blk = lax.broadcasted_iota(jnp.int32, (nblk, ATT_BLOCK), 0)
    kio = lax.broadcasted_iota(jnp.int32, (ATT_BLOCK, ATT_BLOCK), 0)
    qio = lax.broadcasted_iota(jnp.int32, (ATT_BLOCK, ATT_BLOCK), 1)
    kloc = lax.broadcasted_iota(jnp.int32, (ATT_BLOCK, 1), 0).astype(F32)

    outs = []
    for hh in range(2):
        slope = slopes_ref[2 * p + hh]
        in_head = (frow >= hh * HEAD_DIM) & (frow < (hh + 1) * HEAD_DIM)
        qTh = jnp.where(in_head, qT, jnp.zeros_like(qT))

        gate = jnp.dot(kmean_ref[...], qTh.astype(F32), preferred_element_type=F32,
                       precision=lax.Precision.HIGHEST)
        gate = jnp.where(blk < i, gate, NEG_INF)
        cnt = jnp.zeros((nblk, ATT_BLOCK), jnp.int32)
        for m in range(nblk):
            gm = gate[m:m + 1, :]
            beats = (gm > gate) | ((gm == gate) & (blk > m))
            cnt = cnt + beats.astype(jnp.int32)
        sel_ref[hh] = jnp.where((blk < i) & (cnt < MOBA_TOPK), 1.0, 0.0).astype(F32)

        vrows = slice(hh * HEAD_DIM, (hh + 1) * HEAD_DIM)

        s = jnp.dot(k_ref[0, i], qTh, preferred_element_type=F32)
        s = s + slope * kloc
        s = jnp.where(kio <= qio, s, NEG_INF)
        m0 = jnp.max(s, axis=0, keepdims=True)
        pm = jnp.exp(s - m0)
        l0 = jnp.sum(pm, axis=0, keepdims=True)
        acc0 = jnp.dot(vT_ref[0, i, vrows, :], pm.astype(BF16), preferred_element_type=F32)

        def body(n, carry):
            m_run, l_run, acc = carry
            s = jnp.dot(k_ref[0, n], qTh, preferred_element_type=F32)
            s = s + (slope * kloc - slope * ((i - n) * MOBA_BLOCK).astype(F32))
            s = jnp.where(sel_ref[hh, pl.ds(n, 1), :] > 0.0, s, NEG_INF)
            m_new = jnp.maximum(m_run, jnp.max(s, axis=0, keepdims=True))
            alpha = jnp.exp(m_run - m_new)
            pm = jnp.exp(s - m_new)
            l_new = alpha * l_run + jnp.sum(pm, axis=0, keepdims=True)
            acc = alpha * acc + jnp.dot(vT_ref[0, n, vrows, :], pm.astype(BF16), preferred_element_type=F32)
            return m_new, l_new, acc

        _, l_fin, acc = lax.fori_loop(0, i, body, (m0, l0, acc0))
        outs.append(acc / l_fin)

    o_ref[0] = jnp.concatenate(outs, axis=0).T.astype(BF16)


def _moba(slopes, qT, k, vT):
    B, nblk = qT.shape[0], qT.shape[1]
    S = nblk * ATT_BLOCK
    n_pairs = MOBA_W // PAIR_W
    return pl.pallas_call(
        functools.partial(_moba_kernel, nblk=nblk),
        grid=(B, n_pairs, nblk),
        in_specs=[
            pl.BlockSpec(memory_space=pltpu.SMEM),
            pl.BlockSpec((1, 1, PAIR_W, ATT_BLOCK), lambda b, p, i: (b, i, p, 0)),
            pl.BlockSpec((1, nblk, ATT_BLOCK, PAIR_W), lambda b, p, i: (b, 0, 0, p)),
            pl.BlockSpec((1, nblk, PAIR_W, ATT_BLOCK), lambda b, p, i: (b, 0, p, 0)),
        ],
        out_specs=pl.BlockSpec((1, ATT_BLOCK, PAIR_W), lambda b, p, i: (b, i, p)),
        out_shape=jax.ShapeDtypeStruct((B, S, MOBA_W), BF16),
        scratch_shapes=[pltpu.VMEM((nblk, PAIR_W), F32), pltpu.VMEM((2, nblk, ATT_BLOCK), F32)],
        compiler_params=pltpu.CompilerParams(dimension_semantics=("parallel", "parallel", "arbitrary"),
                                             vmem_limit_bytes=VMEM_LIMIT),
        name="moba_attn",
    )(slopes, qT, k, vT)


def _diff_kernel(slopes_ref, lq1_ref, lk1_ref, lq2_ref, lk2_ref, g_ref, qT_ref, k_ref, vT_ref, o_ref,
                 *, lam_init):
    hd = pl.program_id(1)
    i = pl.program_id(2)
    slope = slopes_ref[hd]
    lam = (jnp.exp(jnp.sum(lq1_ref[...] * lk1_ref[...], axis=-1, keepdims=True))
           - jnp.exp(jnp.sum(lq2_ref[...] * lk2_ref[...], axis=-1, keepdims=True)) + lam_init)

    qT = qT_ref[0, 0]
    frow = lax.broadcasted_iota(jnp.int32, (PAIR_W, ATT_BLOCK), 0)
    zero = jnp.zeros_like(qT)
    q_maps = [jnp.where(frow < HEAD_DIM, qT, zero), jnp.where(frow >= HEAD_DIM, qT, zero)]
    kio = lax.broadcasted_iota(jnp.int32, (ATT_BLOCK, ATT_BLOCK), 0)
    qio = lax.broadcasted_iota(jnp.int32, (ATT_BLOCK, ATT_BLOCK), 1)
    kloc = lax.broadcasted_iota(jnp.int32, (ATT_BLOCK, 1), 0).astype(F32)

    kb = k_ref[0, i]
    vb = vT_ref[0, i]
    init = []
    for qm in q_maps:
        s = jnp.dot(kb, qm, preferred_element_type=F32) + slope * kloc
        s = jnp.where(kio <= qio, s, NEG_INF)
        m0 = jnp.max(s, axis=0, keepdims=True)
        pm = jnp.exp(s - m0)
        l0 = jnp.sum(pm, axis=0, keepdims=True)
        acc0 = jnp.dot(vb, pm.astype(BF16), preferred_element_type=F32)
        init += [m0, l0, acc0]

    def body(n, carry):
        kb = k_ref[0, n]
        vb = vT_ref[0, n]
        bias = slope * kloc - slope * ((i - n) * ATT_BLOCK).astype(F32)
        out = []
        for mi, qm in enumerate(q_maps):
            m_run, l_run, acc = carry[3 * mi:3 * mi + 3]
            s = jnp.dot(kb, qm, preferred_element_type=F32) + bias
            m_new = jnp.maximum(m_run, jnp.max(s, axis=0, keepdims=True))
            alpha = jnp.exp(m_run - m_new)
            pm = jnp.exp(s - m_new)
            l_new = alpha * l_run + jnp.sum(pm, axis=0, keepdims=True)
            acc = alpha * acc + jnp.dot(vb, pm.astype(BF16), preferred_element_type=F32)
            out += [m_new, l_new, acc]
        return tuple(out)

    _, l1, acc1, _, l2, acc2 = lax.fori_loop(0, i, body, tuple(init))
    o = acc1 / l1 - lam * (acc2 / l2)
    o = o * lax.rsqrt(jnp.mean(o * o, axis=0, keepdims=True) + RMS_EPS)
    o = o * g_ref[...] * (1.0 - lam_init)
    o_ref[0] = o.T.astype(BF16)


def _diff(slopes, lq1, lk1, lq2, lk2, g_col, qT, k, vT, lam_init):
    B, nblk = qT.shape[0], qT.shape[1]
    S = nblk * ATT_BLOCK
    off = MOBA_W // PAIR_W
    vec = pl.BlockSpec((1, HEAD_DIM), lambda b, h, i: (0, 0))
    return pl.pallas_call(
        functools.partial(_diff_kernel, lam_init=lam_init),
        grid=(B, DIFF_HEADS, nblk),
        in_specs=[
            pl.BlockSpec(memory_space=pltpu.SMEM),
            vec, vec, vec, vec,
            pl.BlockSpec((DIFF_V_DIM, 1), lambda b, h, i: (0, 0)),
            pl.BlockSpec((1, 1, PAIR_W, ATT_BLOCK), lambda b, h, i: (b, i, off + h, 0)),
            pl.BlockSpec((1, nblk, ATT_BLOCK, PAIR_W), lambda b, h, i: (b, 0, 0, off + h)),
            pl.BlockSpec((1, nblk, PAIR_W, ATT_BLOCK), lambda b, h, i: (b, 0, off + h, 0)),
        ],
        out_specs=pl.BlockSpec((1, ATT_BLOCK, PAIR_W), lambda b, h, i: (b, i, h)),
        out_shape=jax.ShapeDtypeStruct((B, S, DIFF_HEADS * DIFF_V_DIM), BF16),
        compiler_params=pltpu.CompilerParams(dimension_semantics=("parallel", "parallel", "parallel"),
                                             vmem_limit_bytes=VMEM_LIMIT),
        name="diff_attn",
    )(slopes, lq1, lk1, lq2, lk2, g_col, qT, k, vT)


def _outproj_kernel(x_ref, om_ref, od_ref, wm_ref, wd_ref, g_ref, o_ref):
    y = (jnp.dot(om_ref[...], wm_ref[...], preferred_element_type=F32)
         + jnp.dot(od_ref[...], wd_ref[...], preferred_element_type=F32))
    o_ref[...] = x_ref[...] + _rms(y) * g_ref[...]


def _outproj(x2, om, od, wm, wd, g):
    T = x2.shape[0]
    const = lambda *shape: pl.BlockSpec(shape, lambda i: (0,) * len(shape), pipeline_mode=pl.Buffered(1))
    half = MOBA_W
    return pl.pallas_call(
        _outproj_kernel,
        grid=(T // TOKEN_TILE,),
        in_specs=[
            pl.BlockSpec((TOKEN_TILE, D_MODEL), lambda i: (i, 0)),
            pl.BlockSpec((TOKEN_TILE, half), lambda i: (i, 0)),
            pl.BlockSpec((TOKEN_TILE, half), lambda i: (i, 0)),
            const(half, D_MODEL),
            const(half, D_MODEL),
            const(1, D_MODEL),
        ],
        out_specs=pl.BlockSpec((TOKEN_TILE, D_MODEL), lambda i: (i, 0)),
        out_shape=jax.ShapeDtypeStruct((T, D_MODEL), F32),
        compiler_params=pltpu.CompilerParams(dimension_semantics=("parallel",), vmem_limit_bytes=VMEM_LIMIT),
        name="mix_out",
    )(x2, om, od, wm, wd, g)


def _alibi_slopes(n):
    return jnp.asarray(2.0 ** (-8.0 * np.arange(1, n + 1) / n), dtype=F32)


def _ffn_weights(w_gate, w_up, w_down):
    chunk = lambda w: w.reshape(D_MODEL, N_FF_CHUNKS, FF_CHUNK).transpose(1, 0, 2)
    wgu = jnp.concatenate([chunk(w_gate), chunk(w_up)], axis=-1).astype(BF16)
    wd = w_down.reshape(N_FF_CHUNKS, FF_CHUNK, D_MODEL).astype(BF16)
    return wgu, wd


def kernel(x, ffn1_pre_g, ffn1_w_gate, ffn1_w_up, ffn1_w_down, ffn1_post_g, mix_pre_g, w_in, lambda_q1, lambda_k1, lambda_q2, lambda_k2, subln_g, w_out, mix_post_g, ffn2_pre_g, ffn2_w_gate, ffn2_w_up, ffn2_w_down, ffn2_post_g):
    B, S, D = x.shape
    depth = w_in.shape[0]
    slopes_m = _alibi_slopes(MOBA_HEADS)
    slopes_d = _alibi_slopes(DIFF_HEADS)
    row = lambda v: v.reshape(1, -1)
    x2 = x.reshape(B * S, D)
    for l in range(depth):
        wgu, wd = _ffn_weights(ffn1_w_gate[l], ffn1_w_up[l], ffn1_w_down[l])
        x2 = _ffn(x2, row(ffn1_pre_g[l]), wgu, wd, row(ffn1_post_g[l]))

        w = w_in[l]
        scale = HEAD_DIM ** -0.5
        wq = jnp.concatenate([w[:, 0:512], w[:, 1536:2048]], axis=1) * scale
        wk = jnp.concatenate([w[:, 512:1024], w[:, 2048:2560]], axis=1)
        wv = jnp.concatenate([w[:, 1024:1536], w[:, 2560:3072]], axis=1)
        qT, k, vT = _proj(x2.reshape(B, S, D), row(mix_pre_g[l]), wq.T.astype(BF16), wk.astype(BF16),
                          wv.T.astype(BF16))
        o_m = _moba(slopes_m, qT, k, vT)
        lam_init = 0.8 - 0.6 * math.exp(-0.3 * l)
        o_d = _diff(slopes_d, row(lambda_q1[l]), row(lambda_k1[l]), row(lambda_q2[l]), row(lambda_k2[l]),
                    subln_g[l].reshape(DIFF_V_DIM, 1), qT, k, vT, lam_init)
        wo = w_out[l].astype(BF16)
        x2 = _outproj(x2, o_m.reshape(B * S, MOBA_W), o_d.reshape(B * S, MOBA_W), wo[:MOBA_W], wo[MOBA_W:],
                      row(mix_post_g[l]))

        wgu, wd = _ffn_weights(ffn2_w_gate[l], ffn2_w_up[l], ffn2_w_down[l])
        x2 = _ffn(x2, row(ffn2_pre_g[l]), wgu, wd, row(ffn2_post_g[l]))
    return x2.reshape(B, S, D)
```

```python
import functools
import math

import jax
import jax.numpy as jnp
import numpy as np
from jax import lax
from jax.experimental import pallas as pl
from jax.experimental.pallas import tpu as pltpu

F32 = jnp.float32
BF16 = jnp.bfloat16

D_MODEL = 1024
HEAD_DIM = 64
MOBA_HEADS = 8
MOBA_BLOCK = 256
MOBA_TOPK = 3
DIFF_HEADS = 4
DIFF_V_DIM = 128
MOBA_W = 512
D_FF = 2816
RMS_EPS = 1e-6

FF_CHUNK = 256
N_FF_CHUNKS = D_FF // FF_CHUNK
TOKEN_TILE = 512
ATT_BLOCK = 256
CHAIN_W = 128
N_CHAINS = MOBA_HEADS + 2 * DIFF_HEADS
AUG_ROWS = 16
CHAINS_PER_STEP = 4
VMEM_LIMIT = 56 * 1024 * 1024

LOG2E = 1.4426950408889634
Q_SCALE = HEAD_DIM ** -0.5 * LOG2E
NEG_INF = float("-inf")
NEG_BIG = -1e30


def _rms(x):
    return x * lax.rsqrt(jnp.mean(x * x, axis=-1, keepdims=True) + RMS_EPS)


def _ffn_kernel(x_ref, gpre_ref, wgu_ref, wd_ref, gpost_ref, o_ref, h_ref, acc_ref):
    x = x_ref[...]
    h_ref[...] = (_rms(x) * gpre_ref[...]).astype(BF16)

    def body(c, carry):
        gu = jnp.dot(h_ref[...], wgu_ref[c], preferred_element_type=F32)
        g = gu[:, :FF_CHUNK]
        u = gu[:, FF_CHUNK:]
        a = (g * jax.nn.sigmoid(g) * u).astype(BF16)
        d = jnp.dot(a, wd_ref[c], preferred_element_type=F32)

        @pl.when(c == 0)
        def _():
            acc_ref[...] = d

        @pl.when(c > 0)
        def _():
            acc_ref[...] += d

        return carry

    lax.fori_loop(0, N_FF_CHUNKS, body, 0)
    y = acc_ref[...]
    o_ref[...] = x_ref[...] + 0.5 * (_rms(y) * gpost_ref[...])


def _ffn(x2, gpre, wgu, wd, gpost):
    T = x2.shape[0]
    const = lambda *shape: pl.BlockSpec(shape, lambda i: (0,) * len(shape), pipeline_mode=pl.Buffered(1))
    return pl.pallas_call(
        _ffn_kernel,
        grid=(T // TOKEN_TILE,),
        in_specs=[
            pl.BlockSpec((TOKEN_TILE, D_MODEL), lambda i: (i, 0)),
            const(1, D_MODEL),
            const(N_FF_CHUNKS, D_MODEL, 2 * FF_CHUNK),
            const(N_FF_CHUNKS, FF_CHUNK, D_MODEL),
            const(1, D_MODEL),
        ],
        out_specs=pl.BlockSpec((TOKEN_TILE, D_MODEL), lambda i: (i, 0)),
        out_shape=jax.ShapeDtypeStruct((T, D_MODEL), F32),
        scratch_shapes=[pltpu.VMEM((TOKEN_TILE, D_MODEL), BF16), pltpu.VMEM((TOKEN_TILE, D_MODEL), F32)],
        compiler_params=pltpu.CompilerParams(dimension_semantics=("parallel",), vmem_limit_bytes=VMEM_LIMIT),
        name="ffn",
    )(x2, gpre, wgu, wd, gpost)


def _proj_kernel(x_ref, g_ref, wqT_ref, wk_ref, wvT_ref, kpat_ref, qT_ref, k_ref, vT_ref):
    h = (_rms(x_ref[0]) * g_ref[...]).astype(BF16)
    nt = (((1,), (1,)), ((), ()))
    kk = jnp.dot(h, wk_ref[...], preferred_element_type=F32)
    qT = (lax.dot_general(wqT_ref[...], h, nt, preferred_element_type=F32) * Q_SCALE).astype(BF16)
    vT = lax.dot_general(wvT_ref[...], h, nt, preferred_element_type=F32).astype(BF16)
    for j in range(TOKEN_TILE // ATT_BLOCK):
        sl = slice(j * ATT_BLOCK, (j + 1) * ATT_BLOCK)
        k_ref[0, j] = (kk[sl, :] + kpat_ref[...]).astype(BF16)
        qT_ref[0, j] = qT[:, sl]
        vT_ref[0, j] = vT[:, sl]


def _proj(x, g, wqT, wk, wvT, kpat):
    B, S, _ = x.shape
    nblk = S // ATT_BLOCK
    per = TOKEN_TILE // ATT_BLOCK
    k_w = N_CHAINS * CHAIN_W
    const = lambda *shape: pl.BlockSpec(shape, lambda b, i: (0,) * len(shape), pipeline_mode=pl.Buffered(1))
    t_spec = pl.BlockSpec((1, per, D_MODEL, ATT_BLOCK), lambda b, i: (b, i, 0, 0))
    t_shape = jax.ShapeDtypeStruct((B, nblk, D_MODEL, ATT_BLOCK), BF16)
    return pl.pallas_call(
        _proj_kernel,
        grid=(B, S // TOKEN_TILE),
        in_specs=[
            pl.BlockSpec((1, TOKEN_TILE, D_MODEL), lambda b, i: (b, i, 0)),
            const(1, D_MODEL),
            const(D_MODEL, D_MODEL),
            const(D_MODEL, k_w),
            const(D_MODEL, D_MODEL),
            const(ATT_BLOCK, k_w),
        ],
        out_specs=[t_spec, pl.BlockSpec((1, per, ATT_BLOCK, k_w), lambda b, i: (b, i, 0, 0)), t_spec],
        out_shape=[t_shape, jax.ShapeDtypeStruct((B, nblk, ATT_BLOCK, k_w), BF16), t_shape],
        compiler_params=pltpu.CompilerParams(dimension_semantics=("parallel", "parallel"),
                                             vmem_limit_bytes=VMEM_LIMIT),
        name="mix_proj",
    )(x, g, wqT, wk, wvT, kpat)


def _split3(t):
    hi = t.astype(BF16).astype(F32)
    r = t - hi
    mid = r.astype(BF16).astype(F32)
    lo = (r - mid).astype(BF16).astype(F32)
    return hi, mid, lo


def _build_bias_rows(base_ref, idx, slope2, nblk):
    rowio = lax.broadcasted_iota(jnp.int32, (AUG_ROWS, ATT_BLOCK), 0)
    is_hi = (rowio == 0) | (rowio == 3) | (rowio == 6)
    is_mid = (rowio == 1) | (rowio == 4) | (rowio == 7)
    for d in range(nblk):
        off = slope2 * (-float(ATT_BLOCK * d))
        t = jnp.where(rowio < 3, off, jnp.where(rowio < 6, 16.0 * slope2, jnp.where(rowio < 9, slope2, 0.0)))
        hi, mid, lo = _split3(t)
        base_ref[idx, d] = jnp.where(is_hi, hi, jnp.where(is_mid, mid, lo))


def _score_rhs(q_rows, aug):
    pad = jnp.zeros((CHAIN_W - HEAD_DIM - AUG_ROWS, ATT_BLOCK), BF16)
    return jnp.concatenate([q_rows, aug, pad], axis=0)


def _value_lhs(v_rows):
    return jnp.concatenate([v_rows, jnp.ones((AUG_ROWS, ATT_BLOCK), BF16)], axis=0)


def _softmax_steps(scores, value_lhs, m_ref, acc_ref, first):
    probs, alphas = [], []
    for c, s in enumerate(scores):
        m_new = jnp.max(s, axis=0, keepdims=True)
        if not first:
            m_run = m_ref[c]
            m_new = jnp.maximum(m_run, m_new)
            alphas.append(jnp.exp2(m_run - m_new))
        m_ref[c] = m_new
        probs.append(jnp.exp2(s - m_new).astype(BF16))
    for c, v in enumerate(value_lhs):
        pv = jnp.dot(v, probs[c], preferred_element_type=F32)
        acc_ref[c] = pv if first else alphas[c] * acc_ref[c] + pv


def _moba_kernel(slopes_ref, qT_ref, k_ref, vT_ref, o_ref, kmean_ref, kms_ref, sel_ref, base_ref, m_ref, acc_ref,
                 *, nblk):
    g = pl.program_id(1)
    i = pl.program_id(2)
    chains = range(CHAINS_PER_STEP)

    def q_rows(c):
        return qT_ref[0, 0, c * HEAD_DIM:(c + 1) * HEAD_DIM, :]

    def k_blk(c, n):
        return k_ref[0, n, :, c * CHAIN_W:(c + 1) * CHAIN_W]

    def v_lhs(c, n):
        return _value_lhs(vT_ref[0, n, c * HEAD_DIM:(c + 1) * HEAD_DIM, :])

    @pl.when(i == 0)
    def _():
        for c in chains:
            _build_bias_rows(base_ref, c, slopes_ref[g * CHAINS_PER_STEP + c] * LOG2E, nblk)
            for n in range(nblk):
                kmean_ref[n:n + 1, :] = jnp.mean(k_blk(c, n).astype(F32), axis=0, keepdims=True)
            hi, mid, lo = _split3(kmean_ref[...])
            kms_ref[c] = jnp.concatenate([hi, mid, lo], axis=0).astype(BF16)

    blk = lax.broadcasted_iota(jnp.int32, (nblk, ATT_BLOCK), 0)
    kio = lax.broadcasted_iota(jnp.int32, (ATT_BLOCK, ATT_BLOCK), 0)
    qio = lax.broadcasted_iota(jnp.int32, (ATT_BLOCK, ATT_BLOCK), 1)
    rowio = lax.broadcasted_iota(jnp.int32, (AUG_ROWS, ATT_BLOCK), 0)
    zero_rows = jnp.zeros((CHAIN_W - HEAD_DIM, ATT_BLOCK), BF16)

    for c in chains:
        x = jnp.dot(kms_ref[c], jnp.concatenate([q_rows(c), zero_rows], axis=0), preferred_element_type=F32)
        gate = x[0:nblk] + x[nblk:2 * nblk] + x[2 * nblk:3 * nblk]
        gate = jnp.where(blk < i, gate, NEG_INF)
        cnt = jnp.zeros((nblk, ATT_BLOCK), jnp.int32)
        for m in range(nblk):
            gm = gate[m:m + 1, :]
            cnt = cnt + jnp.where(gm > gate, 1, 0) + jnp.where(gm == gate, jnp.where(blk > m, 1, 0), 0)
        sel_ref[c] = jnp.where(blk < i, jnp.where(cnt < MOBA_TOPK, 1.0, 0.0), 0.0)

    own = []
    for c in chains:
        s = jnp.dot(k_blk(c, i), _score_rhs(q_rows(c), base_ref[c, 0].astype(BF16)), preferred_element_type=F32)
        own.append(jnp.where(kio <= qio, s, NEG_INF))
    _softmax_steps(own, [v_lhs(c, i) for c in chains], m_ref, acc_ref, first=True)

    def body(n, carry):
        scores = []
        for c in chains:
            base = base_ref[c, i - n]
            picked = sel_ref[c, pl.ds(n, 1), :] > 0.0
            aug = jnp.where(rowio == 0, jnp.where(picked, base, NEG_BIG), base).astype(BF16)
            scores.append(jnp.dot(k_blk(c, n), _score_rhs(q_rows(c), aug), preferred_element_type=F32))
        _softmax_steps(scores, [v_lhs(c, n) for c in chains], m_ref, acc_ref, first=False)
        return carry

    lax.fori_loop(0, i, body, 0)
    for pair in range(CHAINS_PER_STEP // 2):
        outs = []
        for c in (2 * pair, 2 * pair + 1):
            a = acc_ref[c]
            outs.append(a[:HEAD_DIM] / a[HEAD_DIM:HEAD_DIM + 1])
        o_ref[0, :, pair * 2 * HEAD_DIM:(pair + 1) * 2 * HEAD_DIM] = jnp.concatenate(outs, axis=0).T.astype(BF16)


def _moba(slopes, qT, k, vT):
    B, nblk = qT.shape[0], qT.shape[1]
    S = nblk * ATT_BLOCK
    nc = CHAINS_PER_STEP
    width = nc * HEAD_DIM
    return pl.pallas_call(
        functools.partial(_moba_kernel, nblk=nblk),
        grid=(B, MOBA_HEADS // nc, nblk),
        in_specs=[
            pl.BlockSpec(memory_space=pltpu.SMEM),
            pl.BlockSpec((1, 1, width, ATT_BLOCK), lambda b, g, i: (b, i, g, 0)),
            pl.BlockSpec((1, nblk, ATT_BLOCK, nc * CHAIN_W), lambda b, g, i: (b, 0, 0, g)),
            pl.BlockSpec((1, nblk, width, ATT_BLOCK), lambda b, g, i: (b, 0, g, 0)),
        ],
        out_specs=pl.BlockSpec((1, ATT_BLOCK, width), lambda b, g, i: (b, i, g)),
        out_shape=jax.ShapeDtypeStruct((B, S, MOBA_W), BF16),
        scratch_shapes=[
            pltpu.VMEM((nblk, CHAIN_W), F32),
            pltpu.VMEM((nc, 3 * nblk, CHAIN_W), BF16),
            pltpu.VMEM((nc, nblk, ATT_BLOCK), F32),
            pltpu.VMEM((nc, nblk, AUG_ROWS, ATT_BLOCK), F32),
            pltpu.VMEM((nc, 1, ATT_BLOCK), F32),
            pltpu.VMEM((nc, HEAD_DIM + AUG_ROWS, ATT_BLOCK), F32),
        ],
        compiler_params=pltpu.CompilerParams(dimension_semantics=("parallel", "parallel", "arbitrary"),
                                             vmem_limit_bytes=VMEM_LIMIT),
        name="moba_attn",
    )(slopes, qT, k, vT)


def _diff_kernel(slopes_ref, lq1_ref, lk1_ref, lq2_ref, lk2_ref, g_ref, qT_ref, k_ref, vT_ref, o_ref,
                 base_ref, m_ref, acc_ref, *, lam_init, nblk):
    g = pl.program_id(1)
    i = pl.program_id(2)
    n_heads = CHAINS_PER_STEP // 2
    chains = range(CHAINS_PER_STEP)
    lam = (jnp.exp(jnp.sum(lq1_ref[...] * lk1_ref[...], axis=-1, keepdims=True))
           - jnp.exp(jnp.sum(lq2_ref[...] * lk2_ref[...], axis=-1, keepdims=True)) + lam_init)

    def q_rows(c):
        return qT_ref[0, 0, c * HEAD_DIM:(c + 1) * HEAD_DIM, :]

    def k_blk(c, n):
        return k_ref[0, n, :, c * CHAIN_W:(c + 1) * CHAIN_W]

    def v_lhs(c, n):
        hd = c // 2
        return _value_lhs(vT_ref[0, n, hd * DIFF_V_DIM:(hd + 1) * DIFF_V_DIM, :])

    @pl.when(i == 0)
    def _():
        for hd in range(n_heads):
            _build_bias_rows(base_ref, hd, slopes_ref[g * n_heads + hd] * LOG2E, nblk)

    kio = lax.broadcasted_iota(jnp.int32, (ATT_BLOCK, ATT_BLOCK), 0)
    qio = lax.broadcasted_iota(jnp.int32, (ATT_BLOCK, ATT_BLOCK), 1)

    own = []
    for c in chains:
        s = jnp.dot(k_blk(c, i), _score_rhs(q_rows(c), base_ref[c // 2, 0].astype(BF16)),
                    preferred_element_type=F32)
        own.append(jnp.where(kio <= qio, s, NEG_INF))
    _softmax_steps(own, [v_lhs(c, i) for c in chains], m_ref, acc_ref, first=True)

    def body(n, carry):
        scores = [jnp.dot(k_blk(c, n), _score_rhs(q_rows(c), base_ref[c // 2, i - n].astype(BF16)),
                          preferred_element_type=F32) for c in chains]
        _softmax_steps(scores, [v_lhs(c, n) for c in chains], m_ref, acc_ref, first=False)
        return carry

    lax.fori_loop(0, i, body, 0)
    for hd in range(n_heads):
        a1 = acc_ref[2 * hd]
        a2 = acc_ref[2 * hd + 1]
        o = (a1[:DIFF_V_DIM] / a1[DIFF_V_DIM:DIFF_V_DIM + 1]
             - lam * (a2[:DIFF_V_DIM] / a2[DIFF_V_DIM:DIFF_V_DIM + 1]))
        o = o * lax.rsqrt(jnp.mean(o * o, axis=0, keepdims=True) + RMS_EPS)
        o = o * g_ref[...] * (1.0 - lam_init)
        o_ref[0, :, hd * DIFF_V_DIM:(hd + 1) * DIFF_V_DIM] = o.T.astype(BF16)


def _diff(slopes, lq1, lk1, lq2, lk2, g_col, qT, k, vT, lam_init):
    B, nblk = qT.shape[0], qT.shape[1]
    S = nblk * ATT_BLOCK
    nc = CHAINS_PER_STEP
    n_heads = nc // 2
    width = n_heads * DIFF_V_DIM
    off = MOBA_W // width
    vec = pl.BlockSpec((1, HEAD_DIM), lambda b, g, i: (0, 0))
    return pl.pallas_call(
        functools.partial(_diff_kernel, lam_init=lam_init, nblk=nblk),
        grid=(B, DIFF_HEADS // n_heads, nblk),
        in_specs=[
            pl.BlockSpec(memory_space=pltpu.SMEM),
            vec, vec, vec, vec,
            pl.BlockSpec((DIFF_V_DIM, 1), lambda b, g, i: (0, 0)),
            pl.BlockSpec((1, 1, width, ATT_BLOCK), lambda b, g, i: (b, i, off + g, 0)),
            pl.BlockSpec((1, nblk, ATT_BLOCK, nc * CHAIN_W), lambda b, g, i: (b, 0, 0, off + g)),
            pl.BlockSpec((1, nblk, width, ATT_BLOCK), lambda b, g, i: (b, 0, off + g, 0)),
        ],
        out_specs=pl.BlockSpec((1, ATT_BLOCK, width), lambda b, g, i: (b, i, g)),
        out_shape=jax.ShapeDtypeStruct((B, S, DIFF_HEADS * DIFF_V_DIM), BF16),
        scratch_shapes=[
            pltpu.VMEM((n_heads, nblk, AUG_ROWS, ATT_BLOCK), F32),
            pltpu.VMEM((nc, 1, ATT_BLOCK), F32),
            pltpu.VMEM((nc, DIFF_V_DIM + AUG_ROWS, ATT_BLOCK), F32),
        ],
        compiler_params=pltpu.CompilerParams(dimension_semantics=("parallel", "parallel", "arbitrary"),
                                             vmem_limit_bytes=VMEM_LIMIT),
        name="diff_attn",
    )(slopes, lq1, lk1, lq2, lk2, g_col, qT, k, vT)


def _outproj_kernel(x_ref, om_ref, od_ref, wm_ref, wd_ref, g_ref, o_ref):
    y = (jnp.dot(om_ref[...], wm_ref[...], preferred_element_type=F32)
         + jnp.dot(od_ref[...], wd_ref[...], preferred_element_type=F32))
    o_ref[...] = x_ref[...] + _rms(y) * g_ref[...]


def _outproj(x2, om, od, wm, wd, g):
    T = x2.shape[0]
    const = lambda *shape: pl.BlockSpec(shape, lambda i: (0,) * len(shape), pipeline_mode=pl.Buffered(1))
    half = MOBA_W
    return pl.pallas_call(
        _outproj_kernel,
        grid=(T // TOKEN_TILE,),
        in_specs=[
            pl.BlockSpec((TOKEN_TILE, D_MODEL), lambda i: (i, 0)),
            pl.BlockSpec((TOKEN_TILE, half), lambda i: (i, 0)),
            pl.BlockSpec((TOKEN_TILE, half), lambda i: (i, 0)),
            const(half, D_MODEL),
            const(half, D_MODEL),
            const(1, D_MODEL),
        ],
        out_specs=pl.BlockSpec((TOKEN_TILE, D_MODEL), lambda i: (i, 0)),
        out_shape=jax.ShapeDtypeStruct((T, D_MODEL), F32),
        compiler_params=pltpu.CompilerParams(dimension_semantics=("parallel",), vmem_limit_bytes=VMEM_LIMIT),
        name="mix_out",
    )(x2, om, od, wm, wd, g)


def _alibi_slopes(n):
    return jnp.asarray(2.0 ** (-8.0 * np.arange(1, n + 1) / n), dtype=F32)


def _ffn_weights(w_gate, w_up, w_down):
    chunk = lambda w: w.reshape(D_MODEL, N_FF_CHUNKS, FF_CHUNK).transpose(1, 0, 2)
    wgu = jnp.concatenate([chunk(w_gate), chunk(w_up)], axis=-1).astype(BF16)
    wd = w_down.reshape(N_FF_CHUNKS, FF_CHUNK, D_MODEL).astype(BF16)
    return wgu, wd


def _k_bias_pattern():
    r = np.arange(ATT_BLOCK)
    cols = np.zeros((ATT_BLOCK, CHAIN_W), np.float32)
    cols[:, HEAD_DIM:HEAD_DIM + 3] = 1.0
    cols[:, HEAD_DIM + 3:HEAD_DIM + 6] = (r // 16)[:, None]
    cols[:, HEAD_DIM + 6:HEAD_DIM + 9] = (r % 16)[:, None]
    return jnp.asarray(np.tile(cols, (1, N_CHAINS)))


def kernel(x, ffn1_pre_g, ffn1_w_gate, ffn1_w_up, ffn1_w_down, ffn1_post_g, mix_pre_g, w_in, lambda_q1, lambda_k1, lambda_q2, lambda_k2, subln_g, w_out, mix_post_g, ffn2_pre_g, ffn2_w_gate, ffn2_w_up, ffn2_w_down, ffn2_post_g):
    B, S, D = x.shape
    depth = w_in.shape[0]
    slopes_m = _alibi_slopes(MOBA_HEADS)
    slopes_d = _alibi_slopes(DIFF_HEADS)
    kpat = _k_bias_pattern()
    row = lambda v: v.reshape(1, -1)
    x2 = x.reshape(B * S, D)
    for l in range(depth):
        wgu, wd = _ffn_weights(ffn1_w_gate[l], ffn1_w_up[l], ffn1_w_down[l])
        x2 = _ffn(x2, row(ffn1_pre_g[l]), wgu, wd, row(ffn1_post_g[l]))

        w = w_in[l]
        wq = jnp.concatenate([w[:, 0:512], w[:, 1536:2048]], axis=1)
        wk = jnp.concatenate([w[:, 512:1024], w[:, 2048:2560]], axis=1).reshape(D, N_CHAINS, HEAD_DIM)
        wk = jnp.pad(wk, ((0, 0), (0, 0), (0, CHAIN_W - HEAD_DIM))).reshape(D, N_CHAINS * CHAIN_W)
        wv = jnp.concatenate([w[:, 1024:1536], w[:, 2560:3072]], axis=1)
        qT, k, vT = _proj(x2.reshape(B, S, D), row(mix_pre_g[l]), wq.T.astype(BF16), wk.astype(BF16),
                          wv.T.astype(BF16), kpat)
        o_m = _moba(slopes_m, qT, k, vT)
        lam_init = 0.8 - 0.6 * math.exp(-0.3 * l)
        o_d = _diff(slopes_d, row(lambda_q1[l]), row(lambda_k1[l]), row(lambda_q2[l]), row(lambda_k2[l]),
                    subln_g[l].reshape(DIFF_V_DIM, 1), qT, k, vT, lam_init)
        wo = w_out[l].astype(BF16)
        x2 = _outproj(x2, o_m.reshape(B * S, MOBA_W), o_d.reshape(B * S, MOBA_W), wo[:MOBA_W], wo[MOBA_W:],
                      row(mix_post_g[l]))

        wgu, wd = _ffn_weights(ffn2_w_gate[l], ffn2_w_up[l], ffn2_w_down[l])
        x2 = _ffn(x2, row(ffn2_pre_g[l]), wgu, wd, row(ffn2_post_g[l]))
    return x2.reshape(B, S, D)
```

```python
import functools
import math

import jax
import jax.numpy as jnp
import numpy as np
from jax import lax
from jax.experimental import pallas as pl
from jax.experimental.pallas import tpu as pltpu

F32 = jnp.float32
BF16 = jnp.bfloat16

D_MODEL = 1024
HEAD_DIM = 64
MOBA_HEADS = 8
MOBA_BLOCK = 256
MOBA_TOPK = 3
DIFF_HEADS = 4
DIFF_V_DIM = 128
MOBA_W = 512
D_FF = 2816
RMS_EPS = 1e-6

FF_CHUNK = 256
N_FF_CHUNKS = D_FF // FF_CHUNK
TOKEN_TILE = 512
ATT_BLOCK = 256
CHAIN_W = 128
N_CHAINS = MOBA_HEADS + 2 * DIFF_HEADS
AUG_ROWS = 16
CHAINS_PER_STEP = 4
VMEM_LIMIT = 56 * 1024 * 1024

LOG2E = 1.4426950408889634
Q_SCALE = HEAD_DIM ** -0.5 * LOG2E
NEG_INF = float("-inf")
NEG_BIG = -1e30


def _rms(x):
    return x * lax.rsqrt(jnp.mean(x * x, axis=-1, keepdims=True) + RMS_EPS)


def _ffn_kernel(x_ref, gpre_ref, wgu_ref, wd_ref, gpost_ref, o_ref, h_ref, acc_ref):
    x = x_ref[...]
    h_ref[...] = (_rms(x) * gpre_ref[...]).astype(BF16)

    def body(c, carry):
        gu = jnp.dot(h_ref[...], wgu_ref[c], preferred_element_type=F32)
        g = gu[:, :FF_CHUNK]
        u = gu[:, FF_CHUNK:]
        a = (g * jax.nn.sigmoid(g) * u).astype(BF16)
        d = jnp.dot(a, wd_ref[c], preferred_element_type=F32)

        @pl.when(c == 0)
        def _():
            acc_ref[...] = d

        @pl.when(c > 0)
        def _():
            acc_ref[...] += d

        return carry

    lax.fori_loop(0, N_FF_CHUNKS, body, 0)
    y = acc_ref[...]
    o_ref[...] = x_ref[...] + 0.5 * (_rms(y) * gpost_ref[...])


def _ffn(x2, gpre, wgu, wd, gpost):
    T = x2.shape[0]
    const = lambda *shape: pl.BlockSpec(shape, lambda i: (0,) * len(shape), pipeline_mode=pl.Buffered(1))
    return pl.pallas_call(
        _ffn_kernel,
        grid=(T // TOKEN_TILE,),
        in_specs=[
            pl.BlockSpec((TOKEN_TILE, D_MODEL), lambda i: (i, 0)),
            const(1, D_MODEL),
            const(N_FF_CHUNKS, D_MODEL, 2 * FF_CHUNK),
            const(N_FF_CHUNKS, FF_CHUNK, D_MODEL),
            const(1, D_MODEL),
        ],
        out_specs=pl.BlockSpec((TOKEN_TILE, D_MODEL), lambda i: (i, 0)),
        out_shape=jax.ShapeDtypeStruct((T, D_MODEL), F32),
        scratch_shapes=[pltpu.VMEM((TOKEN_TILE, D_MODEL), BF16), pltpu.VMEM((TOKEN_TILE, D_MODEL), F32)],
        compiler_params=pltpu.CompilerParams(dimension_semantics=("parallel",), vmem_limit_bytes=VMEM_LIMIT),
        name="ffn",
    )(x2, gpre, wgu, wd, gpost)


def _proj_kernel(x_ref, g_ref, wqT_ref, wk_ref, wvT_ref, kpat_ref, qT_ref, k_ref, vT_ref):
    h = (_rms(x_ref[0]) * g_ref[...]).astype(BF16)
    nt = (((1,), (1,)), ((), ()))
    kk = jnp.dot(h, wk_ref[...], preferred_element_type=F32)
    qT = (lax.dot_general(wqT_ref[...], h, nt, preferred_element_type=F32) * Q_SCALE).astype(BF16)
    vT = lax.dot_general(wvT_ref[...], h, nt, preferred_element_type=F32).astype(BF16)
    for j in range(TOKEN_TILE // ATT_BLOCK):
        sl = slice(j * ATT_BLOCK, (j + 1) * ATT_BLOCK)
        k_ref[0, j] = (kk[sl, :] + kpat_ref[...]).astype(BF16)
        qT_ref[0, j] = qT[:, sl]
        vT_ref[0, j] = vT[:, sl]


def _proj(x, g, wqT, wk, wvT, kpat):
    B, S, _ = x.shape
    nblk = S // ATT_BLOCK
    per = TOKEN_TILE // ATT_BLOCK
    k_w = N_CHAINS * CHAIN_W
    const = lambda *shape: pl.BlockSpec(shape, lambda b, i: (0,) * len(shape), pipeline_mode=pl.Buffered(1))
    t_spec = pl.BlockSpec((1, per, D_MODEL, ATT_BLOCK), lambda b, i: (b, i, 0, 0))
    t_shape = jax.ShapeDtypeStruct((B, nblk, D_MODEL, ATT_BLOCK), BF16)
    return pl.pallas_call(
        _proj_kernel,
        grid=(B, S // TOKEN_TILE),
        in_specs=[
            pl.BlockSpec((1, TOKEN_TILE, D_MODEL), lambda b, i: (b, i, 0)),
            const(1, D_MODEL),
            const(D_MODEL, D_MODEL),
            const(D_MODEL, k_w),
            const(D_MODEL, D_MODEL),
            const(ATT_BLOCK, k_w),
        ],
        out_specs=[t_spec, pl.BlockSpec((1, per, ATT_BLOCK, k_w), lambda b, i: (b, i, 0, 0)), t_spec],
        out_shape=[t_shape, jax.ShapeDtypeStruct((B, nblk, ATT_BLOCK, k_w), BF16), t_shape],
        compiler_params=pltpu.CompilerParams(dimension_semantics=("parallel", "parallel"),
                                             vmem_limit_bytes=VMEM_LIMIT),
        name="mix_proj",
    )(x, g, wqT, wk, wvT, kpat)


def _split3(t):
    hi = t.astype(BF16).astype(F32)
    r = t - hi
    mid = r.astype(BF16).astype(F32)
    lo = (r - mid).astype(BF16).astype(F32)
    return hi, mid, lo


def _build_bias_rows(base_ref, idx, slope2, nblk):
    rowio = lax.broadcasted_iota(jnp.int32, (AUG_ROWS, ATT_BLOCK), 0)
    is_hi = (rowio == 0) | (rowio == 3) | (rowio == 6)
    is_mid = (rowio == 1) | (rowio == 4) | (rowio == 7)
    for d in range(nblk):
        off = slope2 * (-float(ATT_BLOCK * d))
        t = jnp.where(rowio < 3, off, jnp.where(rowio < 6, 16.0 * slope2, jnp.where(rowio < 9, slope2, 0.0)))
        hi, mid, lo = _split3(t)
        base_ref[idx, d] = jnp.where(is_hi, hi, jnp.where(is_mid, mid, lo))


def _score_rhs(q_rows, aug):
    pad = jnp.zeros((CHAIN_W - HEAD_DIM - AUG_ROWS, ATT_BLOCK), BF16)
    return jnp.concatenate([q_rows, aug, pad], axis=0)


def _value_lhs(v_rows):
    return jnp.concatenate([v_rows, jnp.ones((AUG_ROWS, ATT_BLOCK), BF16)], axis=0)


def _attend_blocks(i, n_chains, score_fn, value_fn, s_ref, p_ref, alpha_ref, m_ref, acc_ref):
    chains = range(n_chains)
    kio = lax.broadcasted_iota(jnp.int32, (ATT_BLOCK, ATT_BLOCK), 0)
    qio = lax.broadcasted_iota(jnp.int32, (ATT_BLOCK, ATT_BLOCK), 1)

    def qk(n, slot):
        for c in chains:
            s_ref[slot, c] = score_fn(c, n)

    def softmax(slot, causal=False):
        for c in chains:
            s = s_ref[slot, c]
            if causal:
                s = jnp.where(kio <= qio, s, NEG_INF)
            m_run = m_ref[c]
            m_new = jnp.maximum(m_run, jnp.max(s, axis=0, keepdims=True))
            alpha_ref[slot, c] = jnp.exp2(m_run - m_new)
            m_ref[c] = m_new
            p_ref[slot, c] = jnp.exp2(s - m_new).astype(BF16)

    def pv(n, slot):
        for c in chains:
            acc_ref[c] = alpha_ref[slot, c] * acc_ref[c] + jnp.dot(value_fn(c, n), p_ref[slot, c],
                                                                   preferred_element_type=F32)

    for c in chains:
        m_ref[c] = jnp.full((1, ATT_BLOCK), NEG_INF, F32)
        acc_ref[c] = jnp.zeros(acc_ref.shape[1:], F32)
        p_ref[1, c] = jnp.zeros((ATT_BLOCK, ATT_BLOCK), BF16)
        alpha_ref[1, c] = jnp.ones((1, ATT_BLOCK), F32)
    qk(0, 0)

    def body(j, carry):
        n = 2 * j
        qk(n + 1, 1)
        softmax(0)
        pv(jnp.maximum(n - 1, 0), 1)

        @pl.when(n + 1 < i)
        def _():
            qk(n + 2, 0)
            softmax(1)
            pv(n, 0)

        return carry

    lax.fori_loop(0, (i + 1) // 2, body, 0)

    @pl.when(i % 2 == 0)
    def _():
        pv(jnp.maximum(i - 1, 0), 1)
        softmax(0, causal=True)
        pv(i, 0)

    @pl.when(i % 2 == 1)
    def _():
        pv(i - 1, 0)
        softmax(1, causal=True)
        pv(i, 1)


def _moba_kernel(slopes_ref, qT_ref, k_ref, vT_ref, o_ref, kmean_ref, kms_ref, sel_ref, base_ref, m_ref, acc_ref,
                 s_ref, p_ref, alpha_ref, *, nblk):
    g = pl.program_id(1)
    i = pl.program_id(2)
    chains = range(CHAINS_PER_STEP)

    def q_rows(c):
        return qT_ref[0, 0, c * HEAD_DIM:(c + 1) * HEAD_DIM, :]

    def k_blk(c, n):
        return k_ref[0, n, :, c * CHAIN_W:(c + 1) * CHAIN_W]

    def v_lhs(c, n):
        return _value_lhs(vT_ref[0, n, c * HEAD_DIM:(c + 1) * HEAD_DIM, :])

    @pl.when(i == 0)
    def _():
        for c in chains:
            _build_bias_rows(base_ref, c, slopes_ref[g * CHAINS_PER_STEP + c] * LOG2E, nblk)
            for n in range(nblk):
                kmean_ref[n:n + 1, :] = jnp.mean(k_blk(c, n).astype(F32), axis=0, keepdims=True)
            hi, mid, lo = _split3(kmean_ref[...])
            kms_ref[c] = jnp.concatenate([hi, mid, lo], axis=0).astype(BF16)

    blk = lax.broadcasted_iota(jnp.int32, (nblk, ATT_BLOCK), 0)
    rowio = lax.broadcasted_iota(jnp.int32, (AUG_ROWS, ATT_BLOCK), 0)
    zero_rows = jnp.zeros((CHAIN_W - HEAD_DIM, ATT_BLOCK), BF16)

    for c in chains:
        x = jnp.dot(kms_ref[c], jnp.concatenate([q_rows(c), zero_rows], axis=0), preferred_element_type=F32)
        gate = x[0:nblk] + x[nblk:2 * nblk] + x[2 * nblk:3 * nblk]
        gate = jnp.where(blk < i, gate, NEG_INF)
        cnt = jnp.zeros((nblk, ATT_BLOCK), jnp.int32)
        for m in range(nblk):
            gm = gate[m:m + 1, :]
            cnt = cnt + jnp.where(gm > gate, 1, 0) + jnp.where(gm == gate, jnp.where(blk > m, 1, 0), 0)
        sel_ref[c] = jnp.where(blk < i, jnp.where(cnt < MOBA_TOPK, 1.0, 0.0), jnp.where(blk == i, 1.0, 0.0))

    def scores(c, n):
        base = base_ref[c, i - n]
        picked = sel_ref[c, pl.ds(n, 1), :] > 0.0
        aug = jnp.where(rowio == 0, jnp.where(picked, base, NEG_BIG), base).astype(BF16)
        return jnp.dot(k_blk(c, n), _score_rhs(q_rows(c), aug), preferred_element_type=F32)

    _attend_blocks(i, CHAINS_PER_STEP, scores, v_lhs, s_ref, p_ref, alpha_ref, m_ref, acc_ref)
    for pair in range(CHAINS_PER_STEP // 2):
        outs = []
        for c in (2 * pair, 2 * pair + 1):
            a = acc_ref[c]
            outs.append(a[:HEAD_DIM] / a[HEAD_DIM:HEAD_DIM + 1])
        o_ref[0, :, pair * 2 * HEAD_DIM:(pair + 1) * 2 * HEAD_DIM] = jnp.concatenate(outs, axis=0).T.astype(BF16)


def _moba(slopes, qT, k, vT):
    B, nblk = qT.shape[0], qT.shape[1]
    S = nblk * ATT_BLOCK
    nc = CHAINS_PER_STEP
    width = nc * HEAD_DIM
    return pl.pallas_call(
        functools.partial(_moba_kernel, nblk=nblk),
        grid=(B, MOBA_HEADS // nc, nblk),
        in_specs=[
            pl.BlockSpec(memory_space=pltpu.SMEM),
            pl.BlockSpec((1, 1, width, ATT_BLOCK), lambda b, g, i: (b, i, g, 0)),
            pl.BlockSpec((1, nblk, ATT_BLOCK, nc * CHAIN_W), lambda b, g, i: (b, 0, 0, g)),
            pl.BlockSpec((1, nblk, width, ATT_BLOCK), lambda b, g, i: (b, 0, g, 0)),
        ],
        out_specs=pl.BlockSpec((1, ATT_BLOCK, width), lambda b, g, i: (b, i, g)),
        out_shape=jax.ShapeDtypeStruct((B, S, MOBA_W), BF16),
        scratch_shapes=[
            pltpu.VMEM((nblk, CHAIN_W), F32),
            pltpu.VMEM((nc, 3 * nblk, CHAIN_W), BF16),
            pltpu.VMEM((nc, nblk, ATT_BLOCK), F32),
            pltpu.VMEM((nc, nblk, AUG_ROWS, ATT_BLOCK), F32),
            pltpu.VMEM((nc, 1, ATT_BLOCK), F32),
            pltpu.VMEM((nc, HEAD_DIM + AUG_ROWS, ATT_BLOCK), F32),
            pltpu.VMEM((2, nc, ATT_BLOCK, ATT_BLOCK), F32),
            pltpu.VMEM((2, nc, ATT_BLOCK, ATT_BLOCK), BF16),
            pltpu.VMEM((2, nc, 1, ATT_BLOCK), F32),
        ],
        compiler_params=pltpu.CompilerParams(dimension_semantics=("parallel", "parallel", "arbitrary"),
                                             vmem_limit_bytes=VMEM_LIMIT),
        name="moba_attn",
    )(slopes, qT, k, vT)


def _diff_kernel(slopes_ref, lq1_ref, lk1_ref, lq2_ref, lk2_ref, g_ref, qT_ref, k_ref, vT_ref, o_ref,
                 base_ref, m_ref, acc_ref, s_ref, p_ref, alpha_ref, *, lam_init, nblk):
    g = pl.program_id(1)
    i = pl.program_id(2)
    n_heads = CHAINS_PER_STEP // 2
    chains = range(CHAINS_PER_STEP)
    lam = (jnp.exp(jnp.sum(lq1_ref[...] * lk1_ref[...], axis=-1, keepdims=True))
           - jnp.exp(jnp.sum(lq2_ref[...] * lk2_ref[...], axis=-1, keepdims=True)) + lam_init)

    def q_rows(c):
        return qT_ref[0, 0, c * HEAD_DIM:(c + 1) * HEAD_DIM, :]

    def k_blk(c, n):
        return k_ref[0, n, :, c * CHAIN_W:(c + 1) * CHAIN_W]

    def v_lhs(c, n):
        hd = c // 2
        return _value_lhs(vT_ref[0, n, hd * DIFF_V_DIM:(hd + 1) * DIFF_V_DIM, :])

    @pl.when(i == 0)
    def _():
        for hd in range(n_heads):
            _build_bias_rows(base_ref, hd, slopes_ref[g * n_heads + hd] * LOG2E, nblk)

    def scores(c, n):
        return jnp.dot(k_blk(c, n), _score_rhs(q_rows(c), base_ref[c // 2, i - n].astype(BF16)),
                       preferred_element_type=F32)

    _attend_blocks(i, CHAINS_PER_STEP, scores, v_lhs, s_ref, p_ref, alpha_ref, m_ref, acc_ref)
    for hd in range(n_heads):
        a1 = acc_ref[2 * hd]
        a2 = acc_ref[2 * hd + 1]
        o = (a1[:DIFF_V_DIM] / a1[DIFF_V_DIM:DIFF_V_DIM + 1]
             - lam * (a2[:DIFF_V_DIM] / a2[DIFF_V_DIM:DIFF_V_DIM + 1]))
        o = o * lax.rsqrt(jnp.mean(o * o, axis=0, keepdims=True) + RMS_EPS)
        o = o * g_ref[...] * (1.0 - lam_init)
        o_ref[0, :, hd * DIFF_V_DIM:(hd + 1) * DIFF_V_DIM] = o.T.astype(BF16)


def _diff(slopes, lq1, lk1, lq2, lk2, g_col, qT, k, vT, lam_init):
    B, nblk = qT.shape[0], qT.shape[1]
    S = nblk * ATT_BLOCK
    nc = CHAINS_PER_STEP
    n_heads = nc // 2
    width = n_heads * DIFF_V_DIM
    off = MOBA_W // width
    vec = pl.BlockSpec((1, HEAD_DIM), lambda b, g, i: (0, 0))
    return pl.pallas_call(
        functools.partial(_diff_kernel, lam_init=lam_init, nblk=nblk),
        grid=(B, DIFF_HEADS // n_heads, nblk),
        in_specs=[
            pl.BlockSpec(memory_space=pltpu.SMEM),
            vec, vec, vec, vec,
            pl.BlockSpec((DIFF_V_DIM, 1), lambda b, g, i: (0, 0)),
            pl.BlockSpec((1, 1, width, ATT_BLOCK), lambda b, g, i: (b, i, off + g, 0)),
            pl.BlockSpec((1, nblk, ATT_BLOCK, nc * CHAIN_W), lambda b, g, i: (b, 0, 0, off + g)),
            pl.BlockSpec((1, nblk, width, ATT_BLOCK), lambda b, g, i: (b, 0, off + g, 0)),
        ],
        out_specs=pl.BlockSpec((1, ATT_BLOCK, width), lambda b, g, i: (b, i, g)),
        out_shape=jax.ShapeDtypeStruct((B, S, DIFF_HEADS * DIFF_V_DIM), BF16),
        scratch_shapes=[
            pltpu.VMEM((n_heads, nblk, AUG_ROWS, ATT_BLOCK), F32),
            pltpu.VMEM((nc, 1, ATT_BLOCK), F32),
            pltpu.VMEM((nc, DIFF_V_DIM + AUG_ROWS, ATT_BLOCK), F32),
            pltpu.VMEM((2, nc, ATT_BLOCK, ATT_BLOCK), F32),
            pltpu.VMEM((2, nc, ATT_BLOCK, ATT_BLOCK), BF16),
            pltpu.VMEM((2, nc, 1, ATT_BLOCK), F32),
        ],
        compiler_params=pltpu.CompilerParams(dimension_semantics=("parallel", "parallel", "arbitrary"),
                                             vmem_limit_bytes=VMEM_LIMIT),
        name="diff_attn",
    )(slopes, lq1, lk1, lq2, lk2, g_col, qT, k, vT)


def _outproj_kernel(x_ref, om_ref, od_ref, wm_ref, wd_ref, g_ref, o_ref):
    y = (jnp.dot(om_ref[...], wm_ref[...], preferred_element_type=F32)
         + jnp.dot(od_ref[...], wd_ref[...], preferred_element_type=F32))
    o_ref[...] = x_ref[...] + _rms(y) * g_ref[...]


def _outproj(x2, om, od, wm, wd, g):
    T = x2.shape[0]
    const = lambda *shape: pl.BlockSpec(shape, lambda i: (0,) * len(shape), pipeline_mode=pl.Buffered(1))
    half = MOBA_W
    return pl.pallas_call(
        _outproj_kernel,
        grid=(T // TOKEN_TILE,),
        in_specs=[
            pl.BlockSpec((TOKEN_TILE, D_MODEL), lambda i: (i, 0)),
            pl.BlockSpec((TOKEN_TILE, half), lambda i: (i, 0)),
            pl.BlockSpec((TOKEN_TILE, half), lambda i: (i, 0)),
            const(half, D_MODEL),
            const(half, D_MODEL),
            const(1, D_MODEL),
        ],
        out_specs=pl.BlockSpec((TOKEN_TILE, D_MODEL), lambda i: (i, 0)),
        out_shape=jax.ShapeDtypeStruct((T, D_MODEL), F32),
        compiler_params=pltpu.CompilerParams(dimension_semantics=("parallel",), vmem_limit_bytes=VMEM_LIMIT),
        name="mix_out",
    )(x2, om, od, wm, wd, g)


def _alibi_slopes(n):
    return jnp.asarray(2.0 ** (-8.0 * np.arange(1, n + 1) / n), dtype=F32)


def _ffn_weights(w_gate, w_up, w_down):
    chunk = lambda w: w.reshape(D_MODEL, N_FF_CHUNKS, FF_CHUNK).transpose(1, 0, 2)
    wgu = jnp.concatenate([chunk(w_gate), chunk(w_up)], axis=-1).astype(BF16)
    wd = w_down.reshape(N_FF_CHUNKS, FF_CHUNK, D_MODEL).astype(BF16)
    return wgu, wd


def _k_bias_pattern():
    r = np.arange(ATT_BLOCK)
    cols = np.zeros((ATT_BLOCK, CHAIN_W), np.float32)
    cols[:, HEAD_DIM:HEAD_DIM + 3] = 1.0
    cols[:, HEAD_DIM + 3:HEAD_DIM + 6] = (r // 16)[:, None]
    cols[:, HEAD_DIM + 6:HEAD_DIM + 9] = (r % 16)[:, None]
    return jnp.asarray(np.tile(cols, (1, N_CHAINS)))


def kernel(x, ffn1_pre_g, ffn1_w_gate, ffn1_w_up, ffn1_w_down, ffn1_post_g, mix_pre_g, w_in, lambda_q1, lambda_k1, lambda_q2, lambda_k2, subln_g, w_out, mix_post_g, ffn2_pre_g, ffn2_w_gate, ffn2_w_up, ffn2_w_down, ffn2_post_g):
    B, S, D = x.shape
    depth = w_in.shape[0]
    slopes_m = _alibi_slopes(MOBA_HEADS)
    slopes_d = _alibi_slopes(DIFF_HEADS)
    kpat = _k_bias_pattern()
    row = lambda v: v.reshape(1, -1)
    x2 = x.reshape(B * S, D)
    for l in range(depth):
        wgu, wd = _ffn_weights(ffn1_w_gate[l], ffn1_w_up[l], ffn1_w_down[l])
        x2 = _ffn(x2, row(ffn1_pre_g[l]), wgu, wd, row(ffn1_post_g[l]))

        w = w_in[l]
        wq = jnp.concatenate([w[:, 0:512], w[:, 1536:2048]], axis=1)
        wk = jnp.concatenate([w[:, 512:1024], w[:, 2048:2560]], axis=1).reshape(D, N_CHAINS, HEAD_DIM)
        wk = jnp.pad(wk, ((0, 0), (0, 0), (0, CHAIN_W - HEAD_DIM))).reshape(D, N_CHAINS * CHAIN_W)
        wv = jnp.concatenate([w[:, 1024:1536], w[:, 2560:3072]], axis=1)
        qT, k, vT = _proj(x2.reshape(B, S, D), row(mix_pre_g[l]), wq.T.astype(BF16), wk.astype(BF16),
                          wv.T.astype(BF16), kpat)
        o_m = _moba(slopes_m, qT, k, vT)
        lam_init = 0.8 - 0.6 * math.exp(-0.3 * l)
        o_d = _diff(slopes_d, row(lambda_q1[l]), row(lambda_k1[l]), row(lambda_q2[l]), row(lambda_k2[l]),
                    subln_g[l].reshape(DIFF_V_DIM, 1), qT, k, vT, lam_init)
        wo = w_out[l].astype(BF16)
        x2 = _outproj(x2, o_m.reshape(B * S, MOBA_W), o_d.reshape(B * S, MOBA_W), wo[:MOBA_W], wo[MOBA_W:],
                      row(mix_post_g[l]))

        wgu, wd = _ffn_weights(ffn2_w_gate[l], ffn2_w_up[l], ffn2_w_down[l])
        x2 = _ffn(x2, row(ffn2_pre_g[l]), wgu, wd, row(ffn2_post_g[l]))
    return x2.reshape(B, S, D)
```

```python
import functools
import math

import jax
import jax.numpy as jnp
import numpy as np
from jax import lax
from jax.experimental import pallas as pl
from jax.experimental.pallas import tpu as pltpu

F32 = jnp.float32
BF16 = jnp.bfloat16

D_MODEL = 1024
HEAD_DIM = 64
MOBA_HEADS = 8
MOBA_BLOCK = 256
MOBA_TOPK = 3
DIFF_HEADS = 4
DIFF_V_DIM = 128
MOBA_W = 512
D_FF = 2816
RMS_EPS = 1e-6

FF_CHUNK = 256
N_FF_CHUNKS = D_FF // FF_CHUNK
TOKEN_TILE = 512
ATT_BLOCK = 256
CHAIN_W = 128
N_CHAINS = MOBA_HEADS + 2 * DIFF_HEADS
AUG_ROWS = 16
CHAINS_PER_STEP = 4
VMEM_LIMIT = 56 * 1024 * 1024

LOG2E = 1.4426950408889634
Q_SCALE = HEAD_DIM ** -0.5 * LOG2E
NEG_INF = float("-inf")
NEG_BIG = -1e30


def _rms(x):
    return x * lax.rsqrt(jnp.mean(x * x, axis=-1, keepdims=True) + RMS_EPS)


def _ffn_kernel(x_ref, gpre_ref, wgu_ref, wd_ref, gpost_ref, o_ref, h_ref, a_ref, acc_ref):
    x = x_ref[...]
    h_ref[...] = (_rms(x) * gpre_ref[...]).astype(BF16)
    acc_ref[...] = jnp.zeros_like(acc_ref)

    def hidden(c):
        gu = jnp.dot(h_ref[...], wgu_ref[c], preferred_element_type=F32)
        g = gu[:, :FF_CHUNK]
        u = gu[:, FF_CHUNK:]
        return (g * jax.nn.sigmoid(g) * u).astype(BF16)

    def project(slot, c):
        acc_ref[...] += jnp.dot(a_ref[slot], wd_ref[c], preferred_element_type=F32)

    a_ref[0] = hidden(0)

    def body(j, carry):
        c = 2 * j
        nxt = hidden(c + 1)
        project(0, c)
        a_ref[1] = nxt
        nxt = hidden(c + 2)
        project(1, c + 1)
        a_ref[0] = nxt
        return carry

    assert N_FF_CHUNKS % 2 == 1
    lax.fori_loop(0, N_FF_CHUNKS // 2, body, 0)
    y = acc_ref[...] + jnp.dot(a_ref[0], wd_ref[N_FF_CHUNKS - 1], preferred_element_type=F32)
    o_ref[...] = x_ref[...] + 0.5 * (_rms(y) * gpost_ref[...])


def _ffn(x2, gpre, wgu, wd, gpost):
    T = x2.shape[0]
    const = lambda *shape: pl.BlockSpec(shape, lambda i: (0,) * len(shape), pipeline_mode=pl.Buffered(1))
    return pl.pallas_call(
        _ffn_kernel,
        grid=(T // TOKEN_TILE,),
        in_specs=[
            pl.BlockSpec((TOKEN_TILE, D_MODEL), lambda i: (i, 0)),
            const(1, D_MODEL),
            const(N_FF_CHUNKS, D_MODEL, 2 * FF_CHUNK),
            const(N_FF_CHUNKS, FF_CHUNK, D_MODEL),
            const(1, D_MODEL),
        ],
        out_specs=pl.BlockSpec((TOKEN_TILE, D_MODEL), lambda i: (i, 0)),
        out_shape=jax.ShapeDtypeStruct((T, D_MODEL), F32),
        scratch_shapes=[pltpu.VMEM((TOKEN_TILE, D_MODEL), BF16), pltpu.VMEM((2, TOKEN_TILE, FF_CHUNK), BF16),
                        pltpu.VMEM((TOKEN_TILE, D_MODEL), F32)],
        compiler_params=pltpu.CompilerParams(dimension_semantics=("parallel",), vmem_limit_bytes=VMEM_LIMIT),
        name="ffn",
    )(x2, gpre, wgu, wd, gpost)


def _proj_kernel(x_ref, g_ref, wqT_ref, wk_ref, wvT_ref, kpat_ref, qT_ref, k_ref, vT_ref):
    h = (_rms(x_ref[0]) * g_ref[...]).astype(BF16)
    nt = (((1,), (1,)), ((), ()))
    kk = jnp.dot(h, wk_ref[...], preferred_element_type=F32)
    qT = (lax.dot_general(wqT_ref[...], h, nt, preferred_element_type=F32) * Q_SCALE).astype(BF16)
    vT = lax.dot_general(wvT_ref[...], h, nt, preferred_element_type=F32).astype(BF16)
    for j in range(TOKEN_TILE // ATT_BLOCK):
        sl = slice(j * ATT_BLOCK, (j + 1) * ATT_BLOCK)
        k_ref[0, j] = (kk[sl, :] + kpat_ref[...]).astype(BF16)
        qT_ref[0, j] = qT[:, sl]
        vT_ref[0, j] = vT[:, sl]


def _proj(x, g, wqT, wk, wvT, kpat):
    B, S, _ = x.shape
    nblk = S // ATT_BLOCK
    per = TOKEN_TILE // ATT_BLOCK
    k_w = N_CHAINS * CHAIN_W
    const = lambda *shape: pl.BlockSpec(shape, lambda b, i: (0,) * len(shape), pipeline_mode=pl.Buffered(1))
    t_spec = pl.BlockSpec((1, per, D_MODEL, ATT_BLOCK), lambda b, i: (b, i, 0, 0))
    t_shape = jax.ShapeDtypeStruct((B, nblk, D_MODEL, ATT_BLOCK), BF16)
    return pl.pallas_call(
        _proj_kernel,
        grid=(B, S // TOKEN_TILE),
        in_specs=[
            pl.BlockSpec((1, TOKEN_TILE, D_MODEL), lambda b, i: (b, i, 0)),
            const(1, D_MODEL),
            const(D_MODEL, D_MODEL),
            const(D_MODEL, k_w),
            const(D_MODEL, D_MODEL),
            const(ATT_BLOCK, k_w),
        ],
        out_specs=[t_spec, pl.BlockSpec((1, per, ATT_BLOCK, k_w), lambda b, i: (b, i, 0, 0)), t_spec],
        out_shape=[t_shape, jax.ShapeDtypeStruct((B, nblk, ATT_BLOCK, k_w), BF16), t_shape],
        compiler_params=pltpu.CompilerParams(dimension_semantics=("parallel", "parallel"),
                                             vmem_limit_bytes=VMEM_LIMIT),
        name="mix_proj",
    )(x, g, wqT, wk, wvT, kpat)


def _split3(t):
    hi = t.astype(BF16).astype(F32)
    r = t - hi
    mid = r.astype(BF16).astype(F32)
    lo = (r - mid).astype(BF16).astype(F32)
    return hi, mid, lo


def _build_bias_rows(base_ref, idx, slope2, nblk):
    rowio = lax.broadcasted_iota(jnp.int32, (AUG_ROWS, ATT_BLOCK), 0)
    is_hi = (rowio == 0) | (rowio == 3) | (rowio == 6)
    is_mid = (rowio == 1) | (rowio == 4) | (rowio == 7)
    for d in range(nblk):
        off = slope2 * (-float(ATT_BLOCK * d))
        t = jnp.where(rowio < 3, off, jnp.where(rowio < 6, 16.0 * slope2, jnp.where(rowio < 9, slope2, 0.0)))
        hi, mid, lo = _split3(t)
        base_ref[idx, d] = jnp.where(is_hi, hi, jnp.where(is_mid, mid, lo))


def _score_rhs(q_rows, aug):
    pad = jnp.zeros((CHAIN_W - HEAD_DIM - AUG_ROWS, ATT_BLOCK), BF16)
    return jnp.concatenate([q_rows, aug, pad], axis=0)


def _value_lhs(v_rows):
    return jnp.concatenate([v_rows, jnp.ones((AUG_ROWS, ATT_BLOCK), BF16)], axis=0)


def _attend_blocks(i, n_chains, score_fn, value_fn, s_ref, p_ref, alpha_ref, m_ref, acc_ref):
    chains = range(n_chains)
    kio = lax.broadcasted_iota(jnp.int32, (ATT_BLOCK, ATT_BLOCK), 0)
    qio = lax.broadcasted_iota(jnp.int32, (ATT_BLOCK, ATT_BLOCK), 1)

    def qk(n, slot):
        for c in chains:
            s_ref[slot, c] = score_fn(c, n)

    def softmax(slot, causal=False):
        for c in chains:
            s = s_ref[slot, c]
            if causal:
                s = jnp.where(kio <= qio, s, NEG_INF)
            m_run = m_ref[c]
            m_new = jnp.maximum(m_run, jnp.max(s, axis=0, keepdims=True))
            alpha_ref[slot, c] = jnp.exp2(m_run - m_new)
            m_ref[c] = m_new
            p_ref[slot, c] = jnp.exp2(s - m_new).astype(BF16)

    def pv(n, slot):
        for c in chains:
            acc_ref[c] = alpha_ref[slot, c] * acc_ref[c] + jnp.dot(value_fn(c, n), p_ref[slot, c],
                                                                   preferred_element_type=F32)

    for c in chains:
        m_ref[c] = jnp.full((1, ATT_BLOCK), NEG_INF, F32)
        acc_ref[c] = jnp.zeros(acc_ref.shape[1:], F32)
        p_ref[1, c] = jnp.zeros((ATT_BLOCK, ATT_BLOCK), BF16)
        alpha_ref[1, c] = jnp.ones((1, ATT_BLOCK), F32)
    qk(0, 0)

    def body(j, carry):
        n = 2 * j
        qk(n + 1, 1)
        softmax(0)
        pv(jnp.maximum(n - 1, 0), 1)

        @pl.when(n + 1 < i)
        def _():
            qk(n + 2, 0)
            softmax(1)
            pv(n, 0)

        return carry

    lax.fori_loop(0, (i + 1) // 2, body, 0)

    @pl.when(i % 2 == 0)
    def _():
        pv(jnp.maximum(i - 1, 0), 1)
        softmax(0, causal=True)
        pv(i, 0)

    @pl.when(i % 2 == 1)
    def _():
        pv(i - 1, 0)
        softmax(1, causal=True)
        pv(i, 1)


def _moba_kernel(slopes_ref, qT_ref, k_ref, vT_ref, o_ref, kmean_ref, kms_ref, sel_ref, base_ref, m_ref, acc_ref,
                 s_ref, p_ref, alpha_ref, *, nblk):
    g = pl.program_id(1)
    i = pl.program_id(2)
    chains = range(CHAINS_PER_STEP)

    def q_rows(c):
        return qT_ref[0, 0, c * HEAD_DIM:(c + 1) * HEAD_DIM, :]

    def k_blk(c, n):
        return k_ref[0, n, :, c * CHAIN_W:(c + 1) * CHAIN_W]

    def v_lhs(c, n):
        return _value_lhs(vT_ref[0, n, c * HEAD_DIM:(c + 1) * HEAD_DIM, :])

    @pl.when(i == 0)
    def _():
        for c in chains:
            _build_bias_rows(base_ref, c, slopes_ref[g * CHAINS_PER_STEP + c] * LOG2E, nblk)
            for n in range(nblk):
                kmean_ref[n:n + 1, :] = jnp.mean(k_blk(c, n).astype(F32), axis=0, keepdims=True)
            hi, mid, lo = _split3(kmean_ref[...])
            kms_ref[c] = jnp.concatenate([hi, mid, lo], axis=0).astype(BF16)

    blk = lax.broadcasted_iota(jnp.int32, (nblk, ATT_BLOCK), 0)
    rowio = lax.broadcasted_iota(jnp.int32, (AUG_ROWS, ATT_BLOCK), 0)
    zero_rows = jnp.zeros((CHAIN_W - HEAD_DIM, ATT_BLOCK), BF16)

    for c in chains:
        x = jnp.dot(kms_ref[c], jnp.concatenate([q_rows(c), zero_rows], axis=0), preferred_element_type=F32)
        gate = x[0:nblk] + x[nblk:2 * nblk] + x[2 * nblk:3 * nblk]
        gate = jnp.where(blk < i, gate, NEG_INF)
        cnt = jnp.zeros((nblk, ATT_BLOCK), jnp.int32)
        for m in range(nblk):
            gm = gate[m:m + 1, :]
            cnt = cnt + jnp.where(gm > gate, 1, 0) + jnp.where(gm == gate, jnp.where(blk > m, 1, 0), 0)
        sel_ref[c] = jnp.where(blk < i, jnp.where(cnt < MOBA_TOPK, 1.0, 0.0), jnp.where(blk == i, 1.0, 0.0))

    def scores(c, n):
        base = base_ref[c, i - n]
        picked = sel_ref[c, pl.ds(n, 1), :] > 0.0
        aug = jnp.where(rowio == 0, jnp.where(picked, base, NEG_BIG), base).astype(BF16)
        return jnp.dot(k_blk(c, n), _score_rhs(q_rows(c), aug), preferred_element_type=F32)

    _attend_blocks(i, CHAINS_PER_STEP, scores, v_lhs, s_ref, p_ref, alpha_ref, m_ref, acc_ref)
    for pair in range(CHAINS_PER_STEP // 2):
        outs = []
        for c in (2 * pair, 2 * pair + 1):
            a = acc_ref[c]
            outs.append(a[:HEAD_DIM] / a[HEAD_DIM:HEAD_DIM + 1])
        o_ref[0, :, pair * 2 * HEAD_DIM:(pair + 1) * 2 * HEAD_DIM] = jnp.concatenate(outs, axis=0).T.astype(BF16)


def _moba(slopes, qT, k, vT):
    B, nblk = qT.shape[0], qT.shape[1]
    S = nblk * ATT_BLOCK
    nc = CHAINS_PER_STEP
    width = nc * HEAD_DIM
    return pl.pallas_call(
        functools.partial(_moba_kernel, nblk=nblk),
        grid=(B, MOBA_HEADS // nc, nblk),
        in_specs=[
            pl.BlockSpec(memory_space=pltpu.SMEM),
            pl.BlockSpec((1, 1, width, ATT_BLOCK), lambda b, g, i: (b, i, g, 0)),
            pl.BlockSpec((1, nblk, ATT_BLOCK, nc * CHAIN_W), lambda b, g, i: (b, 0, 0, g)),
            pl.BlockSpec((1, nblk, width, ATT_BLOCK), lambda b, g, i: (b, 0, g, 0)),
        ],
        out_specs=pl.BlockSpec((1, ATT_BLOCK, width), lambda b, g, i: (b, i, g)),
        out_shape=jax.ShapeDtypeStruct((B, S, MOBA_W), BF16),
        scratch_shapes=[
            pltpu.VMEM((nblk, CHAIN_W), F32),
            pltpu.VMEM((nc, 3 * nblk, CHAIN_W), BF16),
            pltpu.VMEM((nc, nblk, ATT_BLOCK), F32),
            pltpu.VMEM((nc, nblk, AUG_ROWS, ATT_BLOCK), F32),
            pltpu.VMEM((nc, 1, ATT_BLOCK), F32),
            pltpu.VMEM((nc, HEAD_DIM + AUG_ROWS, ATT_BLOCK), F32),
            pltpu.VMEM((2, nc, ATT_BLOCK, ATT_BLOCK), F32),
            pltpu.VMEM((2, nc, ATT_BLOCK, ATT_BLOCK), BF16),
            pltpu.VMEM((2, nc, 1, ATT_BLOCK), F32),
        ],
        compiler_params=pltpu.CompilerParams(dimension_semantics=("parallel", "parallel", "arbitrary"),
                                             vmem_limit_bytes=VMEM_LIMIT),
        name="moba_attn",
    )(slopes, qT, k, vT)


def _diff_kernel(slopes_ref, lq1_ref, lk1_ref, lq2_ref, lk2_ref, g_ref, qT_ref, k_ref, vT_ref, o_ref,
                 base_ref, m_ref, acc_ref, s_ref, p_ref, alpha_ref, *, lam_init, nblk):
    g = pl.program_id(1)
    i = pl.program_id(2)
    n_heads = CHAINS_PER_STEP // 2
    chains = range(CHAINS_PER_STEP)
    lam = (jnp.exp(jnp.sum(lq1_ref[...] * lk1_ref[...], axis=-1, keepdims=True))
           - jnp.exp(jnp.sum(lq2_ref[...] * lk2_ref[...], axis=-1, keepdims=True)) + lam_init)

    def q_rows(c):
        return qT_ref[0, 0, c * HEAD_DIM:(c + 1) * HEAD_DIM, :]

    def k_blk(c, n):
        return k_ref[0, n, :, c * CHAIN_W:(c + 1) * CHAIN_W]

    def v_lhs(c, n):
        hd = c // 2
        return _value_lhs(vT_ref[0, n, hd * DIFF_V_DIM:(hd + 1) * DIFF_V_DIM, :])

    @pl.when(i == 0)
    def _():
        for hd in range(n_heads):
            _build_bias_rows(base_ref, hd, slopes_ref[g * n_heads + hd] * LOG2E, nblk)

    def scores(c, n):
        return jnp.dot(k_blk(c, n), _score_rhs(q_rows(c), base_ref[c // 2, i - n].astype(BF16)),
                       preferred_element_type=F32)

    _attend_blocks(i, CHAINS_PER_STEP, scores, v_lhs, s_ref, p_ref, alpha_ref, m_ref, acc_ref)
    for hd in range(n_heads):
        a1 = acc_ref[2 * hd]
        a2 = acc_ref[2 * hd + 1]
        o = (a1[:DIFF_V_DIM] / a1[DIFF_V_DIM:DIFF_V_DIM + 1]
             - lam * (a2[:DIFF_V_DIM] / a2[DIFF_V_DIM:DIFF_V_DIM + 1]))
        o = o * lax.rsqrt(jnp.mean(o * o, axis=0, keepdims=True) + RMS_EPS)
        o = o * g_ref[...] * (1.0 - lam_init)
        o_ref[0, :, hd * DIFF_V_DIM:(hd + 1) * DIFF_V_DIM] = o.T.astype(BF16)


def _diff(slopes, lq1, lk1, lq2, lk2, g_col, qT, k, vT, lam_init):
    B, nblk = qT.shape[0], qT.shape[1]
    S = nblk * ATT_BLOCK
    nc = CHAINS_PER_STEP
    n_heads = nc // 2
    width = n_heads * DIFF_V_DIM
    off = MOBA_W // width
    vec = pl.BlockSpec((1, HEAD_DIM), lambda b, g, i: (0, 0))
    return pl.pallas_call(
        functools.partial(_diff_kernel, lam_init=lam_init, nblk=nblk),
        grid=(B, DIFF_HEADS // n_heads, nblk),
        in_specs=[
            pl.BlockSpec(memory_space=pltpu.SMEM),
            vec, vec, vec, vec,
            pl.BlockSpec((DIFF_V_DIM, 1), lambda b, g, i: (0, 0)),
            pl.BlockSpec((1, 1, width, ATT_BLOCK), lambda b, g, i: (b, i, off + g, 0)),
            pl.BlockSpec((1, nblk, ATT_BLOCK, nc * CHAIN_W), lambda b, g, i: (b, 0, 0, off + g)),
            pl.BlockSpec((1, nblk, width, ATT_BLOCK), lambda b, g, i: (b, 0, off + g, 0)),
        ],
        out_specs=pl.BlockSpec((1, ATT_BLOCK, width), lambda b, g, i: (b, i, g)),
        out_shape=jax.ShapeDtypeStruct((B, S, DIFF_HEADS * DIFF_V_DIM), BF16),
        scratch_shapes=[
            pltpu.VMEM((n_heads, nblk, AUG_ROWS, ATT_BLOCK), F32),
            pltpu.VMEM((nc, 1, ATT_BLOCK), F32),
            pltpu.VMEM((nc, DIFF_V_DIM + AUG_ROWS, ATT_BLOCK), F32),
            pltpu.VMEM((2, nc, ATT_BLOCK, ATT_BLOCK), F32),
            pltpu.VMEM((2, nc, ATT_BLOCK, ATT_BLOCK), BF16),
            pltpu.VMEM((2, nc, 1, ATT_BLOCK), F32),
        ],
        compiler_params=pltpu.CompilerParams(dimension_semantics=("parallel", "parallel", "arbitrary"),
                                             vmem_limit_bytes=VMEM_LIMIT),
        name="diff_attn",
    )(slopes, lq1, lk1, lq2, lk2, g_col, qT, k, vT)


def _outproj_kernel(x_ref, om_ref, od_ref, wm_ref, wd_ref, g_ref, o_ref):
    y = (jnp.dot(om_ref[...], wm_ref[...], preferred_element_type=F32)
         + jnp.dot(od_ref[...], wd_ref[...], preferred_element_type=F32))
    o_ref[...] = x_ref[...] + _rms(y) * g_ref[...]


def _outproj(x2, om, od, wm, wd, g):
    T = x2.shape[0]
    const = lambda *shape: pl.BlockSpec(shape, lambda i: (0,) * len(shape), pipeline_mode=pl.Buffered(1))
    half = MOBA_W
    return pl.pallas_call(
        _outproj_kernel,
        grid=(T // TOKEN_TILE,),
        in_specs=[
            pl.BlockSpec((TOKEN_TILE, D_MODEL), lambda i: (i, 0)),
            pl.BlockSpec((TOKEN_TILE, half), lambda i: (i, 0)),
            pl.BlockSpec((TOKEN_TILE, half), lambda i: (i, 0)),
            const(half, D_MODEL),
            const(half, D_MODEL),
            const(1, D_MODEL),
        ],
        out_specs=pl.BlockSpec((TOKEN_TILE, D_MODEL), lambda i: (i, 0)),
        out_shape=jax.ShapeDtypeStruct((T, D_MODEL), F32),
        compiler_params=pltpu.CompilerParams(dimension_semantics=("parallel",), vmem_limit_bytes=VMEM_LIMIT),
        name="mix_out",
    )(x2, om, od, wm, wd, g)


def _alibi_slopes(n):
    return jnp.asarray(2.0 ** (-8.0 * np.arange(1, n + 1) / n), dtype=F32)


def _ffn_weights(w_gate, w_up, w_down):
    chunk = lambda w: w.reshape(D_MODEL, N_FF_CHUNKS, FF_CHUNK).transpose(1, 0, 2)
    wgu = jnp.concatenate([chunk(w_gate), chunk(w_up)], axis=-1).astype(BF16)
    wd = w_down.reshape(N_FF_CHUNKS, FF_CHUNK, D_MODEL).astype(BF16)
    return wgu, wd


def _k_bias_pattern():
    r = np.arange(ATT_BLOCK)
    cols = np.zeros((ATT_BLOCK, CHAIN_W), np.float32)
    cols[:, HEAD_DIM:HEAD_DIM + 3] = 1.0
    cols[:, HEAD_DIM + 3:HEAD_DIM + 6] = (r // 16)[:, None]
    cols[:, HEAD_DIM + 6:HEAD_DIM + 9] = (r % 16)[:, None]
    return jnp.asarray(np.tile(cols, (1, N_CHAINS)))


def kernel(x, ffn1_pre_g, ffn1_w_gate, ffn1_w_up, ffn1_w_down, ffn1_post_g, mix_pre_g, w_in, lambda_q1, lambda_k1, lambda_q2, lambda_k2, subln_g, w_out, mix_post_g, ffn2_pre_g, ffn2_w_gate, ffn2_w_up, ffn2_w_down, ffn2_post_g):
    B, S, D = x.shape
    depth = w_in.shape[0]
    slopes_m = _alibi_slopes(MOBA_HEADS)
    slopes_d = _alibi_slopes(DIFF_HEADS)
    kpat = _k_bias_pattern()
    row = lambda v: v.reshape(1, -1)
    x2 = x.reshape(B * S, D)
    for l in range(depth):
        wgu, wd = _ffn_weights(ffn1_w_gate[l], ffn1_w_up[l], ffn1_w_down[l])
        x2 = _ffn(x2, row(ffn1_pre_g[l]), wgu, wd, row(ffn1_post_g[l]))

        w = w_in[l]
        wq = jnp.concatenate([w[:, 0:512], w[:, 1536:2048]], axis=1)
        wk = jnp.concatenate([w[:, 512:1024], w[:, 2048:2560]], axis=1).reshape(D, N_CHAINS, HEAD_DIM)
        wk = jnp.pad(wk, ((0, 0), (0, 0), (0, CHAIN_W - HEAD_DIM))).reshape(D, N_CHAINS * CHAIN_W)
        wv = jnp.concatenate([w[:, 1024:1536], w[:, 2560:3072]], axis=1)
        qT, k, vT = _proj(x2.reshape(B, S, D), row(mix_pre_g[l]), wq.T.astype(BF16), wk.astype(BF16),
                          wv.T.astype(BF16), kpat)
        o_m = _moba(slopes_m, qT, k, vT)
        lam_init = 0.8 - 0.6 * math.exp(-0.3 * l)
        o_d = _diff(slopes_d, row(lambda_q1[l]), row(lambda_k1[l]), row(lambda_q2[l]), row(lambda_k2[l]),
                    subln_g[l].reshape(DIFF_V_DIM, 1), qT, k, vT, lam_init)
        wo = w_out[l].astype(BF16)
        x2 = _outproj(x2, o_m.reshape(B * S, MOBA_W), o_d.reshape(B * S, MOBA_W), wo[:MOBA_W], wo[MOBA_W:],
                      row(mix_post_g[l]))

        wgu, wd = _ffn_weights(ffn2_w_gate[l], ffn2_w_up[l], ffn2_w_down[l])
        x2 = _ffn(x2, row(ffn2_pre_g[l]), wgu, wd, row(ffn2_post_g[l]))
    return x2.reshape(B, S, D)
```

```python
import functools
import math

import jax
import jax.numpy as jnp
import numpy as np
from jax import lax
from jax.experimental import pallas as pl
from jax.experimental.pallas import tpu as pltpu

F32 = jnp.float32
BF16 = jnp.bfloat16

D_MODEL = 1024
HEAD_DIM = 64
MOBA_HEADS = 8
MOBA_BLOCK = 256
MOBA_TOPK = 3
DIFF_HEADS = 4
DIFF_V_DIM = 128
MOBA_W = 512
D_FF = 2816
RMS_EPS = 1e-6

FF_CHUNK = 256
N_FF_CHUNKS = D_FF // FF_CHUNK
TOKEN_TILE = 512
ATT_BLOCK = 256
CHAIN_W = 128
N_CHAINS = MOBA_HEADS + 2 * DIFF_HEADS
AUG_ROWS = 16
CHAINS_PER_STEP = 4
VMEM_LIMIT = 56 * 1024 * 1024

LOG2E = 1.4426950408889634
Q_SCALE = HEAD_DIM ** -0.5 * LOG2E
NEG_INF = float("-inf")
NEG_BIG = -1e30


def _rms(x):
    return x * lax.rsqrt(jnp.mean(x * x, axis=-1, keepdims=True) + RMS_EPS)


def _ffn_kernel(x_ref, gpre_ref, wgu_ref, wd_ref, gpost_ref, o_ref, h_ref, a_ref, acc_ref):
    x = x_ref[...]
    h_ref[...] = (_rms(x) * gpre_ref[...]).astype(BF16)
    acc_ref[...] = jnp.zeros_like(acc_ref)

    def hidden(c):
        gu = jnp.dot(h_ref[...], wgu_ref[c], preferred_element_type=F32)
        g = gu[:, :FF_CHUNK]
        u = gu[:, FF_CHUNK:]
        return (g * jax.nn.sigmoid(g) * u).astype(BF16)

    def project(slot, c):
        acc_ref[...] += jnp.dot(a_ref[slot], wd_ref[c], preferred_element_type=F32)

    a_ref[0] = hidden(0)

    def body(j, carry):
        c = 2 * j
        nxt = hidden(c + 1)
        project(0, c)
        a_ref[1] = nxt
        nxt = hidden(c + 2)
        project(1, c + 1)
        a_ref[0] = nxt
        return carry

    assert N_FF_CHUNKS % 2 == 1
    lax.fori_loop(0, N_FF_CHUNKS // 2, body, 0)
    y = acc_ref[...] + jnp.dot(a_ref[0], wd_ref[N_FF_CHUNKS - 1], preferred_element_type=F32)
    o_ref[...] = x_ref[...] + 0.5 * (_rms(y) * gpost_ref[...])


def _ffn(x2, gpre, wgu, wd, gpost):
    T = x2.shape[0]
    const = lambda *shape: pl.BlockSpec(shape, lambda i: (0,) * len(shape), pipeline_mode=pl.Buffered(1))
    return pl.pallas_call(
        _ffn_kernel,
        grid=(T // TOKEN_TILE,),
        in_specs=[
            pl.BlockSpec((TOKEN_TILE, D_MODEL), lambda i: (i, 0)),
            const(1, D_MODEL),
            const(N_FF_CHUNKS, D_MODEL, 2 * FF_CHUNK),
            const(N_FF_CHUNKS, FF_CHUNK, D_MODEL),
            const(1, D_MODEL),
        ],
        out_specs=pl.BlockSpec((TOKEN_TILE, D_MODEL), lambda i: (i, 0)),
        out_shape=jax.ShapeDtypeStruct((T, D_MODEL), F32),
        scratch_shapes=[pltpu.VMEM((TOKEN_TILE, D_MODEL), BF16), pltpu.VMEM((2, TOKEN_TILE, FF_CHUNK), BF16),
                        pltpu.VMEM((TOKEN_TILE, D_MODEL), F32)],
        compiler_params=pltpu.CompilerParams(dimension_semantics=("parallel",), vmem_limit_bytes=VMEM_LIMIT),
        name="ffn",
    )(x2, gpre, wgu, wd, gpost)


def _proj_kernel(x_ref, g_ref, wqT_ref, wk_ref, wvT_ref, kpat_ref, qT_ref, k_ref, vT_ref):
    h = (_rms(x_ref[0]) * g_ref[...]).astype(BF16)
    nt = (((1,), (1,)), ((), ()))
    kk = jnp.dot(h, wk_ref[...], preferred_element_type=F32)
    qT = (lax.dot_general(wqT_ref[...], h, nt, preferred_element_type=F32) * Q_SCALE).astype(BF16)
    vT = lax.dot_general(wvT_ref[...], h, nt, preferred_element_type=F32).astype(BF16)
    for j in range(TOKEN_TILE // ATT_BLOCK):
        sl = slice(j * ATT_BLOCK, (j + 1) * ATT_BLOCK)
        k_ref[0, j] = (kk[sl, :] + kpat_ref[...]).astype(BF16)
        qT_ref[0, j] = qT[:, sl]
        vT_ref[0, j] = vT[:, sl]


def _proj(x, g, wqT, wk, wvT, kpat):
    B, S, _ = x.shape
    nblk = S // ATT_BLOCK
    per = TOKEN_TILE // ATT_BLOCK
    k_w = N_CHAINS * CHAIN_W
    const = lambda *shape: pl.BlockSpec(shape, lambda b, i: (0,) * len(shape), pipeline_mode=pl.Buffered(1))
    t_spec = pl.BlockSpec((1, per, D_MODEL, ATT_BLOCK), lambda b, i: (b, i, 0, 0))
    t_shape = jax.ShapeDtypeStruct((B, nblk, D_MODEL, ATT_BLOCK), BF16)
    return pl.pallas_call(
        _proj_kernel,
        grid=(B, S // TOKEN_TILE),
        in_specs=[
            pl.BlockSpec((1, TOKEN_TILE, D_MODEL), lambda b, i: (b, i, 0)),
            const(1, D_MODEL),
            const(D_MODEL, D_MODEL),
            const(D_MODEL, k_w),
            const(D_MODEL, D_MODEL),
            const(ATT_BLOCK, k_w),
        ],
        out_specs=[t_spec, pl.BlockSpec((1, per, ATT_BLOCK, k_w), lambda b, i: (b, i, 0, 0)), t_spec],
        out_shape=[t_shape, jax.ShapeDtypeStruct((B, nblk, ATT_BLOCK, k_w), BF16), t_shape],
        compiler_params=pltpu.CompilerParams(dimension_semantics=("parallel", "parallel"),
                                             vmem_limit_bytes=VMEM_LIMIT),
        name="mix_proj",
    )(x, g, wqT, wk, wvT, kpat)


def _split3(t):
    hi = t.astype(BF16).astype(F32)
    r = t - hi
    mid = r.astype(BF16).astype(F32)
    lo = (r - mid).astype(BF16).astype(F32)
    return hi, mid, lo


def _build_bias_rows(base_ref, idx, slope2, nblk):
    rowio = lax.broadcasted_iota(jnp.int32, (AUG_ROWS, ATT_BLOCK), 0)
    is_hi = (rowio == 0) | (rowio == 3) | (rowio == 6)
    is_mid = (rowio == 1) | (rowio == 4) | (rowio == 7)
    for d in range(nblk):
        off = slope2 * (-float(ATT_BLOCK * d))
        t = jnp.where(rowio < 3, off, jnp.where(rowio < 6, 16.0 * slope2, jnp.where(rowio < 9, slope2, 0.0)))
        hi, mid, lo = _split3(t)
        base_ref[idx, d] = jnp.where(is_hi, hi, jnp.where(is_mid, mid, lo))


def _score_rhs(q_rows, aug):
    pad = jnp.zeros((CHAIN_W - HEAD_DIM - AUG_ROWS, ATT_BLOCK), BF16)
    return jnp.concatenate([q_rows, aug, pad], axis=0)


def _value_lhs(v_rows):
    return jnp.concatenate([v_rows, jnp.ones((AUG_ROWS, ATT_BLOCK), BF16)], axis=0)


def _attend_blocks(i, n_chains, score_fn, value_fn, s_refs, p_refs, alpha_refs, m_ref, acc_ref):
    chains = range(n_chains)
    kio = lax.broadcasted_iota(jnp.int32, (ATT_BLOCK, ATT_BLOCK), 0)
    qio = lax.broadcasted_iota(jnp.int32, (ATT_BLOCK, ATT_BLOCK), 1)

    def qk(n, slot):
        for c in chains:
            s_refs[slot][c] = score_fn(c, n)

    def softmax(slot, causal=False):
        for c in chains:
            s = s_refs[slot][c]
            if causal:
                s = jnp.where(kio <= qio, s, NEG_INF)
            m_run = m_ref[c]
            m_new = jnp.maximum(m_run, jnp.max(s, axis=0, keepdims=True))
            alpha_refs[slot][c] = jnp.exp2(m_run - m_new)
            m_ref[c] = m_new
            p_refs[slot][c] = jnp.exp2(s - m_new).astype(BF16)

    def pv(n, slot):
        for c in chains:
            acc_ref[c] = alpha_refs[slot][c] * acc_ref[c] + jnp.dot(value_fn(c, n), p_refs[slot][c],
                                                                    preferred_element_type=F32)

    for c in chains:
        m_ref[c] = jnp.full((1, ATT_BLOCK), NEG_INF, F32)
        acc_ref[c] = jnp.zeros(acc_ref.shape[1:], F32)
        p_refs[1][c] = jnp.zeros((ATT_BLOCK, ATT_BLOCK), BF16)
        alpha_refs[1][c] = jnp.ones((1, ATT_BLOCK), F32)
    qk(0, 0)

    def body(j, carry):
        n = 2 * j
        qk(n + 1, 1)
        softmax(0)
        pv(jnp.maximum(n - 1, 0), 1)

        @pl.when(n + 1 < i)
        def _():
            qk(n + 2, 0)
            softmax(1)
            pv(n, 0)

        return carry

    lax.fori_loop(0, (i + 1) // 2, body, 0)

    @pl.when(i % 2 == 0)
    def _():
        pv(jnp.maximum(i - 1, 0), 1)
        softmax(0, causal=True)
        pv(i, 0)

    @pl.when(i % 2 == 1)
    def _():
        pv(i - 1, 0)
        softmax(1, causal=True)
        pv(i, 1)


def _moba_kernel(slopes_ref, qT_ref, k_ref, vT_ref, o_ref, kmean_ref, kms_ref, sel_ref, base_ref, m_ref, acc_ref,
                 s0_ref, s1_ref, p0_ref, p1_ref, a0_ref, a1_ref, *, nblk):
    g = pl.program_id(1)
    chains = range(CHAINS_PER_STEP)

    def k_blk(c, n):
        return k_ref[0, n, :, c * CHAIN_W:(c + 1) * CHAIN_W]

    def v_lhs(c, n):
        return _value_lhs(vT_ref[0, n, c * HEAD_DIM:(c + 1) * HEAD_DIM, :])

    for c in chains:
        _build_bias_rows(base_ref, c, slopes_ref[g * CHAINS_PER_STEP + c] * LOG2E, nblk)
        for n in range(nblk):
            kmean_ref[n:n + 1, :] = jnp.mean(k_blk(c, n).astype(F32), axis=0, keepdims=True)
        hi, mid, lo = _split3(kmean_ref[...])
        kms_ref[c] = jnp.concatenate([hi, mid, lo], axis=0).astype(BF16)

    blk = lax.broadcasted_iota(jnp.int32, (nblk, ATT_BLOCK), 0)
    rowio = lax.broadcasted_iota(jnp.int32, (AUG_ROWS, ATT_BLOCK), 0)
    zero_rows = jnp.zeros((CHAIN_W - HEAD_DIM, ATT_BLOCK), BF16)

    def query_block(i, carry):
        def q_rows(c):
            return qT_ref[0, i, c * HEAD_DIM:(c + 1) * HEAD_DIM, :]

        for c in chains:
            x = jnp.dot(kms_ref[c], jnp.concatenate([q_rows(c), zero_rows], axis=0), preferred_element_type=F32)
            gate = x[0:nblk] + x[nblk:2 * nblk] + x[2 * nblk:3 * nblk]
            gate = jnp.where(blk < i, gate, NEG_INF)
            cnt = jnp.zeros((nblk, ATT_BLOCK), jnp.int32)
            for m in range(nblk):
                gm = gate[m:m + 1, :]
                cnt = cnt + jnp.where(gm > gate, 1, 0) + jnp.where(gm == gate, jnp.where(blk > m, 1, 0), 0)
            sel_ref[c] = jnp.where(blk < i, jnp.where(cnt < MOBA_TOPK, 1.0, 0.0), jnp.where(blk == i, 1.0, 0.0))

        def scores(c, n):
            base = base_ref[c, i - n]
            picked = sel_ref[c, pl.ds(n, 1), :] > 0.0
            aug = jnp.where(rowio == 0, jnp.where(picked, base, NEG_BIG), base).astype(BF16)
            return jnp.dot(k_blk(c, n), _score_rhs(q_rows(c), aug), preferred_element_type=F32)

        _attend_blocks(i, CHAINS_PER_STEP, scores, v_lhs, (s0_ref, s1_ref), (p0_ref, p1_ref), (a0_ref, a1_ref),
                       m_ref, acc_ref)
        rows = pl.ds(pl.multiple_of(i * ATT_BLOCK, ATT_BLOCK), ATT_BLOCK)
        for pair in range(CHAINS_PER_STEP // 2):
            outs = []
            for c in (2 * pair, 2 * pair + 1):
                a = acc_ref[c]
                outs.append(a[:HEAD_DIM] / a[HEAD_DIM:HEAD_DIM + 1])
            o_ref[0, rows, pair * 2 * HEAD_DIM:(pair + 1) * 2 * HEAD_DIM] = (
                jnp.concatenate(outs, axis=0).T.astype(BF16))
        return carry

    lax.fori_loop(0, nblk, query_block, 0)


def _moba(slopes, qT, k, vT):
    B, nblk = qT.shape[0], qT.shape[1]
    S = nblk * ATT_BLOCK
    nc = CHAINS_PER_STEP
    width = nc * HEAD_DIM
    return pl.pallas_call(
        functools.partial(_moba_kernel, nblk=nblk),
        grid=(B, MOBA_HEADS // nc),
        in_specs=[
            pl.BlockSpec(memory_space=pltpu.SMEM),
            pl.BlockSpec((1, nblk, width, ATT_BLOCK), lambda b, g: (b, 0, g, 0)),
            pl.BlockSpec((1, nblk, ATT_BLOCK, nc * CHAIN_W), lambda b, g: (b, 0, 0, g)),
            pl.BlockSpec((1, nblk, width, ATT_BLOCK), lambda b, g: (b, 0, g, 0)),
        ],
        out_specs=pl.BlockSpec((1, S, width), lambda b, g: (b, 0, g)),
        out_shape=jax.ShapeDtypeStruct((B, S, MOBA_W), BF16),
        scratch_shapes=[
            pltpu.VMEM((nblk, CHAIN_W), F32),
            pltpu.VMEM((nc, 3 * nblk, CHAIN_W), BF16),
            pltpu.VMEM((nc, nblk, ATT_BLOCK), F32),
            pltpu.VMEM((nc, nblk, AUG_ROWS, ATT_BLOCK), F32),
            pltpu.VMEM((nc, 1, ATT_BLOCK), F32),
            pltpu.VMEM((nc, HEAD_DIM + AUG_ROWS, ATT_BLOCK), F32),
            pltpu.VMEM((nc, ATT_BLOCK, ATT_BLOCK), F32), pltpu.VMEM((nc, ATT_BLOCK, ATT_BLOCK), F32),
            pltpu.VMEM((nc, ATT_BLOCK, ATT_BLOCK), BF16), pltpu.VMEM((nc, ATT_BLOCK, ATT_BLOCK), BF16),
            pltpu.VMEM((nc, 1, ATT_BLOCK), F32), pltpu.VMEM((nc, 1, ATT_BLOCK), F32),
        ],
        compiler_params=pltpu.CompilerParams(dimension_semantics=("parallel", "parallel"),
                                             vmem_limit_bytes=VMEM_LIMIT),
        name="moba_attn",
    )(slopes, qT, k, vT)


def _diff_kernel(slopes_ref, lq1_ref, lk1_ref, lq2_ref, lk2_ref, g_ref, qT_ref, k_ref, vT_ref, o_ref,
                 base_ref, m_ref, acc_ref, s0_ref, s1_ref, p0_ref, p1_ref, a0_ref, a1_ref, *, lam_init, nblk):
    g = pl.program_id(1)
    n_heads = CHAINS_PER_STEP // 2
    chains = range(CHAINS_PER_STEP)
    lam = (jnp.exp(jnp.sum(lq1_ref[...] * lk1_ref[...], axis=-1, keepdims=True))
           - jnp.exp(jnp.sum(lq2_ref[...] * lk2_ref[...], axis=-1, keepdims=True)) + lam_init)

    def k_blk(c, n):
        return k_ref[0, n, :, c * CHAIN_W:(c + 1) * CHAIN_W]

    def v_lhs(c, n):
        hd = c // 2
        return _value_lhs(vT_ref[0, n, hd * DIFF_V_DIM:(hd + 1) * DIFF_V_DIM, :])

    for hd in range(n_heads):
        _build_bias_rows(base_ref, hd, slopes_ref[g * n_heads + hd] * LOG2E, nblk)

    def query_block(i, carry):
        def scores(c, n):
            q_rows = qT_ref[0, i, c * HEAD_DIM:(c + 1) * HEAD_DIM, :]
            return jnp.dot(k_blk(c, n), _score_rhs(q_rows, base_ref[c // 2, i - n].astype(BF16)),
                           preferred_element_type=F32)

        _attend_blocks(i, CHAINS_PER_STEP, scores, v_lhs, (s0_ref, s1_ref), (p0_ref, p1_ref), (a0_ref, a1_ref),
                       m_ref, acc_ref)
        rows = pl.ds(pl.multiple_of(i * ATT_BLOCK, ATT_BLOCK), ATT_BLOCK)
        for hd in range(n_heads):
            a1 = acc_ref[2 * hd]
            a2 = acc_ref[2 * hd + 1]
            o = (a1[:DIFF_V_DIM] / a1[DIFF_V_DIM:DIFF_V_DIM + 1]
                 - lam * (a2[:DIFF_V_DIM] / a2[DIFF_V_DIM:DIFF_V_DIM + 1]))
            o = o * lax.rsqrt(jnp.mean(o * o, axis=0, keepdims=True) + RMS_EPS)
            o = o * g_ref[...] * (1.0 - lam_init)
            o_ref[0, rows, hd * DIFF_V_DIM:(hd + 1) * DIFF_V_DIM] = o.T.astype(BF16)
        return carry

    lax.fori_loop(0, nblk, query_block, 0)


def _diff(slopes, lq1, lk1, lq2, lk2, g_col, qT, k, vT, lam_init):
    B, nblk = qT.shape[0], qT.shape[1]
    S = nblk * ATT_BLOCK
    nc = CHAINS_PER_STEP
    n_heads = nc // 2
    width = n_heads * DIFF_V_DIM
    off = MOBA_W // width
    vec = pl.BlockSpec((1, HEAD_DIM), lambda b, g: (0, 0))
    return pl.pallas_call(
        functools.partial(_diff_kernel, lam_init=lam_init, nblk=nblk),
        grid=(B, DIFF_HEADS // n_heads),
        in_specs=[
            pl.BlockSpec(memory_space=pltpu.SMEM),
            vec, vec, vec, vec,
            pl.BlockSpec((DIFF_V_DIM, 1), lambda b, g: (0, 0)),
            pl.BlockSpec((1, nblk, width, ATT_BLOCK), lambda b, g: (b, 0, off + g, 0)),
            pl.BlockSpec((1, nblk, ATT_BLOCK, nc * CHAIN_W), lambda b, g: (b, 0, 0, off + g)),
            pl.BlockSpec((1, nblk, width, ATT_BLOCK), lambda b, g: (b, 0, off + g, 0)),
        ],
        out_specs=pl.BlockSpec((1, S, width), lambda b, g: (b, 0, g)),
        out_shape=jax.ShapeDtypeStruct((B, S, DIFF_HEADS * DIFF_V_DIM), BF16),
        scratch_shapes=[
            pltpu.VMEM((n_heads, nblk, AUG_ROWS, ATT_BLOCK), F32),
            pltpu.VMEM((nc, 1, ATT_BLOCK), F32),
            pltpu.VMEM((nc, DIFF_V_DIM + AUG_ROWS, ATT_BLOCK), F32),
            pltpu.VMEM((nc, ATT_BLOCK, ATT_BLOCK), F32), pltpu.VMEM((nc, ATT_BLOCK, ATT_BLOCK), F32),
            pltpu.VMEM((nc, ATT_BLOCK, ATT_BLOCK), BF16), pltpu.VMEM((nc, ATT_BLOCK, ATT_BLOCK), BF16),
            pltpu.VMEM((nc, 1, ATT_BLOCK), F32), pltpu.VMEM((nc, 1, ATT_BLOCK), F32),
        ],
        compiler_params=pltpu.CompilerParams(dimension_semantics=("parallel", "parallel"),
                                             vmem_limit_bytes=VMEM_LIMIT),
        name="diff_attn",
    )(slopes, lq1, lk1, lq2, lk2, g_col, qT, k, vT)


def _outproj_kernel(x_ref, om_ref, od_ref, wm_ref, wd_ref, g_ref, o_ref):
    y = (jnp.dot(om_ref[...], wm_ref[...], preferred_element_type=F32)
         + jnp.dot(od_ref[...], wd_ref[...], preferred_element_type=F32))
    o_ref[...] = x_ref[...] + _rms(y) * g_ref[...]


def _outproj(x2, om, od, wm, wd, g):
    T = x2.shape[0]
    const = lambda *shape: pl.BlockSpec(shape, lambda i: (0,) * len(shape), pipeline_mode=pl.Buffered(1))
    half = MOBA_W
    return pl.pallas_call(
        _outproj_kernel,
        grid=(T // TOKEN_TILE,),
        in_specs=[
            pl.BlockSpec((TOKEN_TILE, D_MODEL), lambda i: (i, 0)),
            pl.BlockSpec((TOKEN_TILE, half), lambda i: (i, 0)),
            pl.BlockSpec((TOKEN_TILE, half), lambda i: (i, 0)),
            const(half, D_MODEL),
            const(half, D_MODEL),
            const(1, D_MODEL),
        ],
        out_specs=pl.BlockSpec((TOKEN_TILE, D_MODEL), lambda i: (i, 0)),
        out_shape=jax.ShapeDtypeStruct((T, D_MODEL), F32),
        compiler_params=pltpu.CompilerParams(dimension_semantics=("parallel",), vmem_limit_bytes=VMEM_LIMIT),
        name="mix_out",
    )(x2, om, od, wm, wd, g)


def _alibi_slopes(n):
    return jnp.asarray(2.0 ** (-8.0 * np.arange(1, n + 1) / n), dtype=F32)


def _ffn_weights(w_gate, w_up, w_down):
    chunk = lambda w: w.reshape(D_MODEL, N_FF_CHUNKS, FF_CHUNK).transpose(1, 0, 2)
    wgu = jnp.concatenate([chunk(w_gate), chunk(w_up)], axis=-1).astype(BF16)
    wd = w_down.reshape(N_FF_CHUNKS, FF_CHUNK, D_MODEL).astype(BF16)
    return wgu, wd


def _k_bias_pattern():
    r = np.arange(ATT_BLOCK)
    cols = np.zeros((ATT_BLOCK, CHAIN_W), np.float32)
    cols[:, HEAD_DIM:HEAD_DIM + 3] = 1.0
    cols[:, HEAD_DIM + 3:HEAD_DIM + 6] = (r // 16)[:, None]
    cols[:, HEAD_DIM + 6:HEAD_DIM + 9] = (r % 16)[:, None]
    return jnp.asarray(np.tile(cols, (1, N_CHAINS)))


def kernel(x, ffn1_pre_g, ffn1_w_gate, ffn1_w_up, ffn1_w_down, ffn1_post_g, mix_pre_g, w_in, lambda_q1, lambda_k1, lambda_q2, lambda_k2, subln_g, w_out, mix_post_g, ffn2_pre_g, ffn2_w_gate, ffn2_w_up, ffn2_w_down, ffn2_post_g):
    B, S, D = x.shape
    depth = w_in.shape[0]
    slopes_m = _alibi_slopes(MOBA_HEADS)
    slopes_d = _alibi_slopes(DIFF_HEADS)
    kpat = _k_bias_pattern()
    row = lambda v: v.reshape(1, -1)
    x2 = x.reshape(B * S, D)
    for l in range(depth):
        wgu, wd = _ffn_weights(ffn1_w_gate[l], ffn1_w_up[l], ffn1_w_down[l])
        x2 = _ffn(x2, row(ffn1_pre_g[l]), wgu, wd, row(ffn1_post_g[l]))

        w = w_in[l]
        wq = jnp.concatenate([w[:, 0:512], w[:, 1536:2048]], axis=1)
        wk = jnp.concatenate([w[:, 512:1024], w[:, 2048:2560]], axis=1).reshape(D, N_CHAINS, HEAD_DIM)
        wk = jnp.pad(wk, ((0, 0), (0, 0), (0, CHAIN_W - HEAD_DIM))).reshape(D, N_CHAINS * CHAIN_W)
        wv = jnp.concatenate([w[:, 1024:1536], w[:, 2560:3072]], axis=1)
        qT, k, vT = _proj(x2.reshape(B, S, D), row(mix_pre_g[l]), wq.T.astype(BF16), wk.astype(BF16),
                          wv.T.astype(BF16), kpat)
        o_m = _moba(slopes_m, qT, k, vT)
        lam_init = 0.8 - 0.6 * math.exp(-0.3 * l)
        o_d = _diff(slopes_d, row(lambda_q1[l]), row(lambda_k1[l]), row(lambda_q2[l]), row(lambda_k2[l]),
                    subln_g[l].reshape(DIFF_V_DIM, 1), qT, k, vT, lam_init)
        wo = w_out[l].astype(BF16)
        x2 = _outproj(x2, o_m.reshape(B * S, MOBA_W), o_d.reshape(B * S, MOBA_W), wo[:MOBA_W], wo[MOBA_W:],
                      row(mix_post_g[l]))

        wgu, wd = _ffn_weights(ffn2_w_gate[l], ffn2_w_up[l], ffn2_w_down[l])
        x2 = _ffn(x2, row(ffn2_pre_g[l]), wgu, wd, row(ffn2_post_g[l]))
    return x2.reshape(B, S, D)
```

```python
import functools
import math

import jax
import jax.numpy as jnp
import numpy as np
from jax import lax
from jax.experimental import pallas as pl
from jax.experimental.pallas import tpu as pltpu

F32 = jnp.float32
BF16 = jnp.bfloat16

D_MODEL = 1024
HEAD_DIM = 64
MOBA_HEADS = 8
MOBA_BLOCK = 256
MOBA_TOPK = 3
DIFF_HEADS = 4
DIFF_V_DIM = 128
MOBA_W = 512
D_FF = 2816
RMS_EPS = 1e-6

FF_CHUNK = 256
N_FF_CHUNKS = D_FF // FF_CHUNK
TOKEN_TILE = 512
ATT_BLOCK = 256
CHAIN_W = 128
N_CHAINS = MOBA_HEADS + 2 * DIFF_HEADS
AUG_ROWS = 16
CHAINS_PER_STEP = 4
VMEM_LIMIT = 56 * 1024 * 1024

LOG2E = 1.4426950408889634
Q_SCALE = HEAD_DIM ** -0.5 * LOG2E
NEG_INF = float("-inf")
NEG_BIG = -1e30


def _rms(x):
    return x * lax.rsqrt(jnp.mean(x * x, axis=-1, keepdims=True) + RMS_EPS)


def _ffn_kernel(x_ref, gpre_ref, wgu_ref, wd_ref, gpost_ref, o_ref, h_ref, a_ref, acc_ref):
    x = x_ref[...]
    h_ref[...] = (_rms(x) * gpre_ref[...]).astype(BF16)
    acc_ref[...] = jnp.zeros_like(acc_ref)

    def hidden(c):
        gu = jnp.dot(h_ref[...], wgu_ref[c], preferred_element_type=F32)
        g = gu[:, :FF_CHUNK]
        u = gu[:, FF_CHUNK:]
        return (g * jax.nn.sigmoid(g) * u).astype(BF16)

    def project(slot, c):
        acc_ref[...] += jnp.dot(a_ref[slot], wd_ref[c], preferred_element_type=F32)

    a_ref[0] = hidden(0)

    def body(j, carry):
        c = 2 * j
        nxt = hidden(c + 1)
        project(0, c)
        a_ref[1] = nxt
        nxt = hidden(c + 2)
        project(1, c + 1)
        a_ref[0] = nxt
        return carry

    assert N_FF_CHUNKS % 2 == 1
    lax.fori_loop(0, N_FF_CHUNKS // 2, body, 0)
    y = acc_ref[...] + jnp.dot(a_ref[0], wd_ref[N_FF_CHUNKS - 1], preferred_element_type=F32)
    o_ref[...] = x_ref[...] + 0.5 * (_rms(y) * gpost_ref[...])


def _ffn(x2, gpre, wgu, wd, gpost):
    T = x2.shape[0]
    const = lambda *shape: pl.BlockSpec(shape, lambda i: (0,) * len(shape), pipeline_mode=pl.Buffered(1))
    return pl.pallas_call(
        _ffn_kernel,
        grid=(T // TOKEN_TILE,),
        in_specs=[
            pl.BlockSpec((TOKEN_TILE, D_MODEL), lambda i: (i, 0)),
            const(1, D_MODEL),
            const(N_FF_CHUNKS, D_MODEL, 2 * FF_CHUNK),
            const(N_FF_CHUNKS, FF_CHUNK, D_MODEL),
            const(1, D_MODEL),
        ],
        out_specs=pl.BlockSpec((TOKEN_TILE, D_MODEL), lambda i: (i, 0)),
        out_shape=jax.ShapeDtypeStruct((T, D_MODEL), F32),
        scratch_shapes=[pltpu.VMEM((TOKEN_TILE, D_MODEL), BF16), pltpu.VMEM((2, TOKEN_TILE, FF_CHUNK), BF16),
                        pltpu.VMEM((TOKEN_TILE, D_MODEL), F32)],
        compiler_params=pltpu.CompilerParams(dimension_semantics=("parallel",), vmem_limit_bytes=VMEM_LIMIT),
        name="ffn",
    )(x2, gpre, wgu, wd, gpost)


def _mix_ffn_kernel(x_ref, om_ref, od_ref, wom_ref, wod_ref, gmix_ref, gpre_ref, wgu_ref, wd_ref, gpost_ref,
                    o_ref, h_ref, a_ref, acc_ref):
    y = (jnp.dot(om_ref[...], wom_ref[...], preferred_element_type=F32)
         + jnp.dot(od_ref[...], wod_ref[...], preferred_element_type=F32))
    o_ref[...] = x_ref[...] + _rms(y) * gmix_ref[...]
    _ffn_kernel(o_ref, gpre_ref, wgu_ref, wd_ref, gpost_ref, o_ref, h_ref, a_ref, acc_ref)


def _mix_ffn(x2, om, od, wom, wod, gmix, gpre, wgu, wd, gpost):
    T = x2.shape[0]
    const = lambda *shape: pl.BlockSpec(shape, lambda i: (0,) * len(shape), pipeline_mode=pl.Buffered(1))
    tile = lambda w: pl.BlockSpec((TOKEN_TILE, w), lambda i: (i, 0))
    return pl.pallas_call(
        _mix_ffn_kernel,
        grid=(T // TOKEN_TILE,),
        in_specs=[
            tile(D_MODEL), tile(MOBA_W), tile(MOBA_W),
            const(MOBA_W, D_MODEL), const(MOBA_W, D_MODEL), const(1, D_MODEL),
            const(1, D_MODEL),
            const(N_FF_CHUNKS, D_MODEL, 2 * FF_CHUNK),
            const(N_FF_CHUNKS, FF_CHUNK, D_MODEL),
            const(1, D_MODEL),
        ],
        out_specs=tile(D_MODEL),
        out_shape=jax.ShapeDtypeStruct((T, D_MODEL), F32),
        scratch_shapes=[pltpu.VMEM((TOKEN_TILE, D_MODEL), BF16), pltpu.VMEM((2, TOKEN_TILE, FF_CHUNK), BF16),
                        pltpu.VMEM((TOKEN_TILE, D_MODEL), F32)],
        compiler_params=pltpu.CompilerParams(dimension_semantics=("parallel",), vmem_limit_bytes=VMEM_LIMIT),
        name="mix_out_ffn",
    )(x2, om, od, wom, wod, gmix, gpre, wgu, wd, gpost)


def _proj_kernel(x_ref, g_ref, wqT_ref, wk_ref, wvT_ref, kpat_ref, qT_ref, k_ref, vT_ref):
    h = (_rms(x_ref[0]) * g_ref[...]).astype(BF16)
    nt = (((1,), (1,)), ((), ()))
    kk = jnp.dot(h, wk_ref[...], preferred_element_type=F32)
    qT = (lax.dot_general(wqT_ref[...], h, nt, preferred_element_type=F32) * Q_SCALE).astype(BF16)
    vT = lax.dot_general(wvT_ref[...], h, nt, preferred_element_type=F32).astype(BF16)
    for j in range(TOKEN_TILE // ATT_BLOCK):
        sl = slice(j * ATT_BLOCK, (j + 1) * ATT_BLOCK)
        k_ref[0, j] = (kk[sl, :] + kpat_ref[...]).astype(BF16)
        qT_ref[0, j] = qT[:, sl]
        vT_ref[0, j] = vT[:, sl]


def _proj(x, g, wqT, wk, wvT, kpat):
    B, S, _ = x.shape
    nblk = S // ATT_BLOCK
    per = TOKEN_TILE // ATT_BLOCK
    k_w = N_CHAINS * CHAIN_W
    const = lambda *shape: pl.BlockSpec(shape, lambda b, i: (0,) * len(shape), pipeline_mode=pl.Buffered(1))
    t_spec = pl.BlockSpec((1, per, D_MODEL, ATT_BLOCK), lambda b, i: (b, i, 0, 0))
    t_shape = jax.ShapeDtypeStruct((B, nblk, D_MODEL, ATT_BLOCK), BF16)
    return pl.pallas_call(
        _proj_kernel,
        grid=(B, S // TOKEN_TILE),
        in_specs=[
            pl.BlockSpec((1, TOKEN_TILE, D_MODEL), lambda b, i: (b, i, 0)),
            const(1, D_MODEL),
            const(D_MODEL, D_MODEL),
            const(D_MODEL, k_w),
            const(D_MODEL, D_MODEL),
            const(ATT_BLOCK, k_w),
        ],
        out_specs=[t_spec, pl.BlockSpec((1, per, ATT_BLOCK, k_w), lambda b, i: (b, i, 0, 0)), t_spec],
        out_shape=[t_shape, jax.ShapeDtypeStruct((B, nblk, ATT_BLOCK, k_w), BF16), t_shape],
        compiler_params=pltpu.CompilerParams(dimension_semantics=("parallel", "parallel"),
                                             vmem_limit_bytes=VMEM_LIMIT),
        name="mix_proj",
    )(x, g, wqT, wk, wvT, kpat)


def _split3(t):
    hi = t.astype(BF16).astype(F32)
    r = t - hi
    mid = r.astype(BF16).astype(F32)
    lo = (r - mid).astype(BF16).astype(F32)
    return hi, mid, lo


def _build_bias_rows(base_ref, idx, slope2, nblk):
    rowio = lax.broadcasted_iota(jnp.int32, (AUG_ROWS, ATT_BLOCK), 0)
    is_hi = (rowio == 0) | (rowio == 3) | (rowio == 6)
    is_mid = (rowio == 1) | (rowio == 4) | (rowio == 7)
    for d in range(nblk):
        off = slope2 * (-float(ATT_BLOCK * d))
        t = jnp.where(rowio < 3, off, jnp.where(rowio < 6, 16.0 * slope2, jnp.where(rowio < 9, slope2, 0.0)))
        hi, mid, lo = _split3(t)
        base_ref[idx, d] = jnp.where(is_hi, hi, jnp.where(is_mid, mid, lo))


def _score_rhs(q_rows, aug):
    pad = jnp.zeros((CHAIN_W - HEAD_DIM - AUG_ROWS, ATT_BLOCK), BF16)
    return jnp.concatenate([q_rows, aug, pad], axis=0)


def _value_lhs(v_rows):
    return jnp.concatenate([v_rows, jnp.ones((AUG_ROWS, ATT_BLOCK), BF16)], axis=0)


def _attend_blocks(i, n_chains, score_fn, value_fn, s_refs, p_refs, alpha_refs, m_ref, acc_ref):
    chains = range(n_chains)
    kio = lax.broadcasted_iota(jnp.int32, (ATT_BLOCK, ATT_BLOCK), 0)
    qio = lax.broadcasted_iota(jnp.int32, (ATT_BLOCK, ATT_BLOCK), 1)

    def qk(n, slot):
        for c in chains:
            s_refs[slot][c] = score_fn(c, n)

    def softmax(slot, causal=False):
        for c in chains:
            s = s_refs[slot][c]
            if causal:
                s = jnp.where(kio <= qio, s, NEG_INF)
            m_run = m_ref[c]
            m_new = jnp.maximum(m_run, jnp.max(s, axis=0, keepdims=True))
            alpha_refs[slot][c] = jnp.exp2(m_run - m_new)
            m_ref[c] = m_new
            p_refs[slot][c] = jnp.exp2(s - m_new).astype(BF16)

    def pv(n, slot):
        for c in chains:
            acc_ref[c] = alpha_refs[slot][c] * acc_ref[c] + jnp.dot(value_fn(c, n), p_refs[slot][c],
                                                                    preferred_element_type=F32)

    for c in chains:
        m_ref[c] = jnp.full((1, ATT_BLOCK), NEG_INF, F32)
        acc_ref[c] = jnp.zeros(acc_ref.shape[1:], F32)
        p_refs[1][c] = jnp.zeros((ATT_BLOCK, ATT_BLOCK), BF16)
        alpha_refs[1][c] = jnp.ones((1, ATT_BLOCK), F32)
    qk(0, 0)

    def body(j, carry):
        n = 2 * j
        qk(n + 1, 1)
        softmax(0)
        pv(jnp.maximum(n - 1, 0), 1)

        @pl.when(n + 1 < i)
        def _():
            qk(n + 2, 0)
            softmax(1)
            pv(n, 0)

        return carry

    lax.fori_loop(0, (i + 1) // 2, body, 0)

    @pl.when(i % 2 == 0)
    def _():
        pv(jnp.maximum(i - 1, 0), 1)
        softmax(0, causal=True)
        pv(i, 0)

    @pl.when(i % 2 == 1)
    def _():
        pv(i - 1, 0)
        softmax(1, causal=True)
        pv(i, 1)


def _moba_kernel(slopes_ref, qT_ref, k_ref, vT_ref, o_ref, kmean_ref, kms_ref, sel_ref, base_ref, m_ref, acc_ref,
                 s0_ref, s1_ref, p0_ref, p1_ref, a0_ref, a1_ref, *, nblk):
    g = pl.program_id(1)
    chains = range(CHAINS_PER_STEP)

    def k_blk(c, n):
        return k_ref[0, n, :, c * CHAIN_W:(c + 1) * CHAIN_W]

    def v_lhs(c, n):
        return _value_lhs(vT_ref[0, n, c * HEAD_DIM:(c + 1) * HEAD_DIM, :])

    for c in chains:
        _build_bias_rows(base_ref, c, slopes_ref[g * CHAINS_PER_STEP + c] * LOG2E, nblk)
        for n in range(nblk):
            kmean_ref[n:n + 1, :] = jnp.mean(k_blk(c, n).astype(F32), axis=0, keepdims=True)
        hi, mid, lo = _split3(kmean_ref[...])
        kms_ref[c] = jnp.concatenate([hi, mid, lo], axis=0).astype(BF16)

    blk = lax.broadcasted_iota(jnp.int32, (nblk, ATT_BLOCK), 0)
    rowio = lax.broadcasted_iota(jnp.int32, (AUG_ROWS, ATT_BLOCK), 0)
    zero_rows = jnp.zeros((CHAIN_W - HEAD_DIM, ATT_BLOCK), BF16)

    def query_block(i, carry):
        def q_rows(c):
            return qT_ref[0, i, c * HEAD_DIM:(c + 1) * HEAD_DIM, :]

        for c in chains:
            x = jnp.dot(kms_ref[c], jnp.concatenate([q_rows(c), zero_rows], axis=0), preferred_element_type=F32)
            gate = x[0:nblk] + x[nblk:2 * nblk] + x[2 * nblk:3 * nblk]
            gate = jnp.where(blk < i, gate, NEG_INF)
            cnt = jnp.zeros((nblk, ATT_BLOCK), jnp.int32)
            for m in range(nblk):
                gm = gate[m:m + 1, :]
                cnt = cnt + jnp.where(gm > gate, 1, 0) + jnp.where(gm == gate, jnp.where(blk > m, 1, 0), 0)
            sel_ref[c] = jnp.where(blk < i, jnp.where(cnt < MOBA_TOPK, 1.0, 0.0), jnp.where(blk == i, 1.0, 0.0))

        def scores(c, n):
            base = base_ref[c, i - n]
            picked = sel_ref[c, pl.ds(n, 1), :] > 0.0
            aug = jnp.where(rowio == 0, jnp.where(picked, base, NEG_BIG), base).astype(BF16)
            return jnp.dot(k_blk(c, n), _score_rhs(q_rows(c), aug), preferred_element_type=F32)

        _attend_blocks(i, CHAINS_PER_STEP, scores, v_lhs, (s0_ref, s1_ref), (p0_ref, p1_ref), (a0_ref, a1_ref),
                       m_ref, acc_ref)
        rows = pl.ds(pl.multiple_of(i * ATT_BLOCK, ATT_BLOCK), ATT_BLOCK)
        for pair in range(CHAINS_PER_STEP // 2):
            outs = []
            for c in (2 * pair, 2 * pair + 1):
                a = acc_ref[c]
                outs.append(a[:HEAD_DIM] / a[HEAD_DIM:HEAD_DIM + 1])
            o_ref[0, rows, pair * 2 * HEAD_DIM:(pair + 1) * 2 * HEAD_DIM] = (
                jnp.concatenate(outs, axis=0).T.astype(BF16))
        return carry

    lax.fori_loop(0, nblk, query_block, 0)


def _moba(slopes, qT, k, vT):
    B, nblk = qT.shape[0], qT.shape[1]
    S = nblk * ATT_BLOCK
    nc = CHAINS_PER_STEP
    width = nc * HEAD_DIM
    return pl.pallas_call(
        functools.partial(_moba_kernel, nblk=nblk),
        grid=(B, MOBA_HEADS // nc),
        in_specs=[
            pl.BlockSpec(memory_space=pltpu.SMEM),
            pl.BlockSpec((1, nblk, width, ATT_BLOCK), lambda b, g: (b, 0, g, 0)),
            pl.BlockSpec((1, nblk, ATT_BLOCK, nc * CHAIN_W), lambda b, g: (b, 0, 0, g)),
            pl.BlockSpec((1, nblk, width, ATT_BLOCK), lambda b, g: (b, 0, g, 0)),
        ],
        out_specs=pl.BlockSpec((1, S, width), lambda b, g: (b, 0, g)),
        out_shape=jax.ShapeDtypeStruct((B, S, MOBA_W), BF16),
        scratch_shapes=[
            pltpu.VMEM((nblk, CHAIN_W), F32),
            pltpu.VMEM((nc, 3 * nblk, CHAIN_W), BF16),
            pltpu.VMEM((nc, nblk, ATT_BLOCK), F32),
            pltpu.VMEM((nc, nblk, AUG_ROWS, ATT_BLOCK), F32),
            pltpu.VMEM((nc, 1, ATT_BLOCK), F32),
            pltpu.VMEM((nc, HEAD_DIM + AUG_ROWS, ATT_BLOCK), F32),
            pltpu.VMEM((nc, ATT_BLOCK, ATT_BLOCK), F32), pltpu.VMEM((nc, ATT_BLOCK, ATT_BLOCK), F32),
            pltpu.VMEM((nc, ATT_BLOCK, ATT_BLOCK), BF16), pltpu.VMEM((nc, ATT_BLOCK, ATT_BLOCK), BF16),
            pltpu.VMEM((nc, 1, ATT_BLOCK), F32), pltpu.VMEM((nc, 1, ATT_BLOCK), F32),
        ],
        compiler_params=pltpu.CompilerParams(dimension_semantics=("parallel", "parallel"),
                                             vmem_limit_bytes=VMEM_LIMIT),
        name="moba_attn",
    )(slopes, qT, k, vT)


def _diff_kernel(slopes_ref, lq1_ref, lk1_ref, lq2_ref, lk2_ref, g_ref, qT_ref, k_ref, vT_ref, o_ref,
                 base_ref, m_ref, acc_ref, s0_ref, s1_ref, p0_ref, p1_ref, a0_ref, a1_ref, *, lam_init, nblk):
    g = pl.program_id(1)
    n_heads = CHAINS_PER_STEP // 2
    chains = range(CHAINS_PER_STEP)
    lam = (jnp.exp(jnp.sum(lq1_ref[...] * lk1_ref[...], axis=-1, keepdims=True))
           - jnp.exp(jnp.sum(lq2_ref[...] * lk2_ref[...], axis=-1, keepdims=True)) + lam_init)

    def k_blk(c, n):
        return k_ref[0, n, :, c * CHAIN_W:(c + 1) * CHAIN_W]

    def v_lhs(c, n):
        hd = c // 2
        return _value_lhs(vT_ref[0, n, hd * DIFF_V_DIM:(hd + 1) * DIFF_V_DIM, :])

    for hd in range(n_heads):
        _build_bias_rows(base_ref, hd, slopes_ref[g * n_heads + hd] * LOG2E, nblk)

    def query_block(i, carry):
        def scores(c, n):
            q_rows = qT_ref[0, i, c * HEAD_DIM:(c + 1) * HEAD_DIM, :]
            return jnp.dot(k_blk(c, n), _score_rhs(q_rows, base_ref[c // 2, i - n].astype(BF16)),
                           preferred_element_type=F32)

        _attend_blocks(i, CHAINS_PER_STEP, scores, v_lhs, (s0_ref, s1_ref), (p0_ref, p1_ref), (a0_ref, a1_ref),
                       m_ref, acc_ref)
        rows = pl.ds(pl.multiple_of(i * ATT_BLOCK, ATT_BLOCK), ATT_BLOCK)
        for hd in range(n_heads):
            a1 = acc_ref[2 * hd]
            a2 = acc_ref[2 * hd + 1]
            o = (a1[:DIFF_V_DIM] / a1[DIFF_V_DIM:DIFF_V_DIM + 1]
                 - lam * (a2[:DIFF_V_DIM] / a2[DIFF_V_DIM:DIFF_V_DIM + 1]))
            o = o * lax.rsqrt(jnp.mean(o * o, axis=0, keepdims=True) + RMS_EPS)
            o = o * g_ref[...] * (1.0 - lam_init)
            o_ref[0, rows, hd * DIFF_V_DIM:(hd + 1) * DIFF_V_DIM] = o.T.astype(BF16)
        return carry

    lax.fori_loop(0, nblk, query_block, 0)


def _diff(slopes, lq1, lk1, lq2, lk2, g_col, qT, k, vT, lam_init):
    B, nblk = qT.shape[0], qT.shape[1]
    S = nblk * ATT_BLOCK
    nc = CHAINS_PER_STEP
    n_heads = nc // 2
    width = n_heads * DIFF_V_DIM
    off = MOBA_W // width
    vec = pl.BlockSpec((1, HEAD_DIM), lambda b, g: (0, 0))
    return pl.pallas_call(
        functools.partial(_diff_kernel, lam_init=lam_init, nblk=nblk),
        grid=(B, DIFF_HEADS // n_heads),
        in_specs=[
            pl.BlockSpec(memory_space=pltpu.SMEM),
            vec, vec, vec, vec,
            pl.BlockSpec((DIFF_V_DIM, 1), lambda b, g: (0, 0)),
            pl.BlockSpec((1, nblk, width, ATT_BLOCK), lambda b, g: (b, 0, off + g, 0)),
            pl.BlockSpec((1, nblk, ATT_BLOCK, nc * CHAIN_W), lambda b, g: (b, 0, 0, off + g)),
            pl.BlockSpec((1, nblk, width, ATT_BLOCK), lambda b, g: (b, 0, off + g, 0)),
        ],
        out_specs=pl.BlockSpec((1, S, width), lambda b, g: (b, 0, g)),
        out_shape=jax.ShapeDtypeStruct((B, S, DIFF_HEADS * DIFF_V_DIM), BF16),
        scratch_shapes=[
            pltpu.VMEM((n_heads, nblk, AUG_ROWS, ATT_BLOCK), F32),
            pltpu.VMEM((nc, 1, ATT_BLOCK), F32),
            pltpu.VMEM((nc, DIFF_V_DIM + AUG_ROWS, ATT_BLOCK), F32),
            pltpu.VMEM((nc, ATT_BLOCK, ATT_BLOCK), F32), pltpu.VMEM((nc, ATT_BLOCK, ATT_BLOCK), F32),
            pltpu.VMEM((nc, ATT_BLOCK, ATT_BLOCK), BF16), pltpu.VMEM((nc, ATT_BLOCK, ATT_BLOCK), BF16),
            pltpu.VMEM((nc, 1, ATT_BLOCK), F32), pltpu.VMEM((nc, 1, ATT_BLOCK), F32),
        ],
        compiler_params=pltpu.CompilerParams(dimension_semantics=("parallel", "parallel"),
                                             vmem_limit_bytes=VMEM_LIMIT),
        name="diff_attn",
    )(slopes, lq1, lk1, lq2, lk2, g_col, qT, k, vT)


def _alibi_slopes(n):
    return jnp.asarray(2.0 ** (-8.0 * np.arange(1, n + 1) / n), dtype=F32)


def _ffn_weights(w_gate, w_up, w_down):
    chunk = lambda w: w.reshape(D_MODEL, N_FF_CHUNKS, FF_CHUNK).transpose(1, 0, 2)
    wgu = jnp.concatenate([chunk(w_gate), chunk(w_up)], axis=-1).astype(BF16)
    wd = w_down.reshape(N_FF_CHUNKS, FF_CHUNK, D_MODEL).astype(BF16)
    return wgu, wd


def _k_bias_pattern():
    r = np.arange(ATT_BLOCK)
    cols = np.zeros((ATT_BLOCK, CHAIN_W), np.float32)
    cols[:, HEAD_DIM:HEAD_DIM + 3] = 1.0
    cols[:, HEAD_DIM + 3:HEAD_DIM + 6] = (r // 16)[:, None]
    cols[:, HEAD_DIM + 6:HEAD_DIM + 9] = (r % 16)[:, None]
    return jnp.asarray(np.tile(cols, (1, N_CHAINS)))


def kernel(x, ffn1_pre_g, ffn1_w_gate, ffn1_w_up, ffn1_w_down, ffn1_post_g, mix_pre_g, w_in, lambda_q1, lambda_k1, lambda_q2, lambda_k2, subln_g, w_out, mix_post_g, ffn2_pre_g, ffn2_w_gate, ffn2_w_up, ffn2_w_down, ffn2_post_g):
    B, S, D = x.shape
    depth = w_in.shape[0]
    slopes_m = _alibi_slopes(MOBA_HEADS)
    slopes_d = _alibi_slopes(DIFF_HEADS)
    kpat = _k_bias_pattern()
    row = lambda v: v.reshape(1, -1)
    x2 = x.reshape(B * S, D)
    for l in range(depth):
        wgu, wd = _ffn_weights(ffn1_w_gate[l], ffn1_w_up[l], ffn1_w_down[l])
        x2 = _ffn(x2, row(ffn1_pre_g[l]), wgu, wd, row(ffn1_post_g[l]))

        w = w_in[l]
        wq = jnp.concatenate([w[:, 0:512], w[:, 1536:2048]], axis=1)
        wk = jnp.concatenate([w[:, 512:1024], w[:, 2048:2560]], axis=1).reshape(D, N_CHAINS, HEAD_DIM)
        wk = jnp.pad(wk, ((0, 0), (0, 0), (0, CHAIN_W - HEAD_DIM))).reshape(D, N_CHAINS * CHAIN_W)
        wv = jnp.concatenate([w[:, 1024:1536], w[:, 2560:3072]], axis=1)
        qT, k, vT = _proj(x2.reshape(B, S, D), row(mix_pre_g[l]), wq.T.astype(BF16), wk.astype(BF16),
                          wv.T.astype(BF16), kpat)
        o_m = _moba(slopes_m, qT, k, vT)
        lam_init = 0.8 - 0.6 * math.exp(-0.3 * l)
        o_d = _diff(slopes_d, row(lambda_q1[l]), row(lambda_k1[l]), row(lambda_q2[l]), row(lambda_k2[l]),
                    subln_g[l].reshape(DIFF_V_DIM, 1), qT, k, vT, lam_init)
        wo = w_out[l].astype(BF16)
        wgu, wd = _ffn_weights(ffn2_w_gate[l], ffn2_w_up[l], ffn2_w_down[l])
        x2 = _mix_ffn(x2, o_m.reshape(B * S, MOBA_W), o_d.reshape(B * S, MOBA_W), wo[:MOBA_W], wo[MOBA_W:],
                      row(mix_post_g[l]), row(ffn2_pre_g[l]), wgu, wd, row(ffn2_post_g[l]))
    return x2.reshape(B, S, D)
```

```python
import functools
import math

import jax
import jax.numpy as jnp
import numpy as np
from jax import lax
from jax.experimental import pallas as pl
from jax.experimental.pallas import tpu as pltpu

F32 = jnp.float32
BF16 = jnp.bfloat16

D_MODEL = 1024
HEAD_DIM = 64
MOBA_HEADS = 8
MOBA_BLOCK = 256
MOBA_TOPK = 3
DIFF_HEADS = 4
DIFF_V_DIM = 128
MOBA_W = 512
D_FF = 2816
RMS_EPS = 1e-6

FF_CHUNK = 256
N_FF_CHUNKS = D_FF // FF_CHUNK
TOKEN_TILE = 512
ATT_BLOCK = 256
CHAIN_W = 128
N_CHAINS = MOBA_HEADS + 2 * DIFF_HEADS
AUG_ROWS = 16
CHAINS_PER_STEP = 4
MOBA_CHAINS_PER_STEP = 8
VMEM_LIMIT = 56 * 1024 * 1024

LOG2E = 1.4426950408889634
Q_SCALE = HEAD_DIM ** -0.5 * LOG2E
NEG_INF = float("-inf")
NEG_BIG = -1e30


def _rms(x):
    return x * lax.rsqrt(jnp.mean(x * x, axis=-1, keepdims=True) + RMS_EPS)


def _ffn_kernel(x_ref, gpre_ref, wgu_ref, wd_ref, gpost_ref, o_ref, h_ref, a_ref, acc_ref):
    x = x_ref[...]
    h_ref[...] = (_rms(x) * gpre_ref[...]).astype(BF16)
    acc_ref[...] = jnp.zeros_like(acc_ref)

    def hidden(c):
        gu = jnp.dot(h_ref[...], wgu_ref[c], preferred_element_type=F32)
        g = gu[:, :FF_CHUNK]
        u = gu[:, FF_CHUNK:]
        return (g * jax.nn.sigmoid(g) * u).astype(BF16)

    def project(slot, c):
        acc_ref[...] += jnp.dot(a_ref[slot], wd_ref[c], preferred_element_type=F32)

    a_ref[0] = hidden(0)

    def body(j, carry):
        c = 2 * j
        nxt = hidden(c + 1)
        project(0, c)
        a_ref[1] = nxt
        nxt = hidden(c + 2)
        project(1, c + 1)
        a_ref[0] = nxt
        return carry

    assert N_FF_CHUNKS % 2 == 1
    lax.fori_loop(0, N_FF_CHUNKS // 2, body, 0)
    y = acc_ref[...] + jnp.dot(a_ref[0], wd_ref[N_FF_CHUNKS - 1], preferred_element_type=F32)
    o_ref[...] = x_ref[...] + 0.5 * (_rms(y) * gpost_ref[...])


def _ffn(x2, gpre, wgu, wd, gpost):
    T = x2.shape[0]
    const = lambda *shape: pl.BlockSpec(shape, lambda i: (0,) * len(shape), pipeline_mode=pl.Buffered(1))
    return pl.pallas_call(
        _ffn_kernel,
        grid=(T // TOKEN_TILE,),
        in_specs=[
            pl.BlockSpec((TOKEN_TILE, D_MODEL), lambda i: (i, 0)),
            const(1, D_MODEL),
            const(N_FF_CHUNKS, D_MODEL, 2 * FF_CHUNK),
            const(N_FF_CHUNKS, FF_CHUNK, D_MODEL),
            const(1, D_MODEL),
        ],
        out_specs=pl.BlockSpec((TOKEN_TILE, D_MODEL), lambda i: (i, 0)),
        out_shape=jax.ShapeDtypeStruct((T, D_MODEL), F32),
        scratch_shapes=[pltpu.VMEM((TOKEN_TILE, D_MODEL), BF16), pltpu.VMEM((2, TOKEN_TILE, FF_CHUNK), BF16),
                        pltpu.VMEM((TOKEN_TILE, D_MODEL), F32)],
        compiler_params=pltpu.CompilerParams(dimension_semantics=("parallel",), vmem_limit_bytes=VMEM_LIMIT),
        name="ffn",
    )(x2, gpre, wgu, wd, gpost)


def _mix_ffn_kernel(x_ref, om_ref, od_ref, wom_ref, wod_ref, gmix_ref, gpre_ref, wgu_ref, wd_ref, gpost_ref,
                    o_ref, h_ref, a_ref, acc_ref):
    y = (jnp.dot(om_ref[...], wom_ref[...], preferred_element_type=F32)
         + jnp.dot(od_ref[...], wod_ref[...], preferred_element_type=F32))
    o_ref[...] = x_ref[...] + _rms(y) * gmix_ref[...]
    _ffn_kernel(o_ref, gpre_ref, wgu_ref, wd_ref, gpost_ref, o_ref, h_ref, a_ref, acc_ref)


def _mix_ffn(x2, om, od, wom, wod, gmix, gpre, wgu, wd, gpost):
    T = x2.shape[0]
    const = lambda *shape: pl.BlockSpec(shape, lambda i: (0,) * len(shape), pipeline_mode=pl.Buffered(1))
    tile = lambda w: pl.BlockSpec((TOKEN_TILE, w), lambda i: (i, 0))
    return pl.pallas_call(
        _mix_ffn_kernel,
        grid=(T // TOKEN_TILE,),
        in_specs=[
            tile(D_MODEL), tile(MOBA_W), tile(MOBA_W),
            const(MOBA_W, D_MODEL), const(MOBA_W, D_MODEL), const(1, D_MODEL),
            const(1, D_MODEL),
            const(N_FF_CHUNKS, D_MODEL, 2 * FF_CHUNK),
            const(N_FF_CHUNKS, FF_CHUNK, D_MODEL),
            const(1, D_MODEL),
        ],
        out_specs=tile(D_MODEL),
        out_shape=jax.ShapeDtypeStruct((T, D_MODEL), F32),
        scratch_shapes=[pltpu.VMEM((TOKEN_TILE, D_MODEL), BF16), pltpu.VMEM((2, TOKEN_TILE, FF_CHUNK), BF16),
                        pltpu.VMEM((TOKEN_TILE, D_MODEL), F32)],
        compiler_params=pltpu.CompilerParams(dimension_semantics=("parallel",), vmem_limit_bytes=VMEM_LIMIT),
        name="mix_out_ffn",
    )(x2, om, od, wom, wod, gmix, gpre, wgu, wd, gpost)


def _proj_kernel(x_ref, g_ref, wqT_ref, wk_ref, wvT_ref, kpat_ref, qT_ref, k_ref, vT_ref):
    h = (_rms(x_ref[0]) * g_ref[...]).astype(BF16)
    nt = (((1,), (1,)), ((), ()))
    kk = jnp.dot(h, wk_ref[...], preferred_element_type=F32)
    qT = (lax.dot_general(wqT_ref[...], h, nt, preferred_element_type=F32) * Q_SCALE).astype(BF16)
    vT = lax.dot_general(wvT_ref[...], h, nt, preferred_element_type=F32).astype(BF16)
    for j in range(TOKEN_TILE // ATT_BLOCK):
        sl = slice(j * ATT_BLOCK, (j + 1) * ATT_BLOCK)
        k_ref[0, j] = (kk[sl, :] + kpat_ref[...]).astype(BF16)
        qT_ref[0, j] = qT[:, sl]
        vT_ref[0, j] = vT[:, sl]


def _proj(x, g, wqT, wk, wvT, kpat):
    B, S, _ = x.shape
    nblk = S // ATT_BLOCK
    per = TOKEN_TILE // ATT_BLOCK
    k_w = N_CHAINS * CHAIN_W
    const = lambda *shape: pl.BlockSpec(shape, lambda b, i: (0,) * len(shape), pipeline_mode=pl.Buffered(1))
    t_spec = pl.BlockSpec((1, per, D_MODEL, ATT_BLOCK), lambda b, i: (b, i, 0, 0))
    t_shape = jax.ShapeDtypeStruct((B, nblk, D_MODEL, ATT_BLOCK), BF16)
    return pl.pallas_call(
        _proj_kernel,
        grid=(B, S // TOKEN_TILE),
        in_specs=[
            pl.BlockSpec((1, TOKEN_TILE, D_MODEL), lambda b, i: (b, i, 0)),
            const(1, D_MODEL),
            const(D_MODEL, D_MODEL),
            const(D_MODEL, k_w),
            const(D_MODEL, D_MODEL),
            const(ATT_BLOCK, k_w),
        ],
        out_specs=[t_spec, pl.BlockSpec((1, per, ATT_BLOCK, k_w), lambda b, i: (b, i, 0, 0)), t_spec],
        out_shape=[t_shape, jax.ShapeDtypeStruct((B, nblk, ATT_BLOCK, k_w), BF16), t_shape],
        compiler_params=pltpu.CompilerParams(dimension_semantics=("parallel", "parallel"),
                                             vmem_limit_bytes=VMEM_LIMIT),
        name="mix_proj",
    )(x, g, wqT, wk, wvT, kpat)


def _split3(t):
    hi = t.astype(BF16).astype(F32)
    r = t - hi
    mid = r.astype(BF16).astype(F32)
    lo = (r - mid).astype(BF16).astype(F32)
    return hi, mid, lo


def _build_bias_rows(base_ref, idx, slope2, nblk):
    rowio = lax.broadcasted_iota(jnp.int32, (AUG_ROWS, ATT_BLOCK), 0)
    is_hi = (rowio == 0) | (rowio == 3) | (rowio == 6)
    is_mid = (rowio == 1) | (rowio == 4) | (rowio == 7)
    for d in range(nblk):
        off = slope2 * (-float(ATT_BLOCK * d))
        t = jnp.where(rowio < 3, off, jnp.where(rowio < 6, 16.0 * slope2, jnp.where(rowio < 9, slope2, 0.0)))
        hi, mid, lo = _split3(t)
        base_ref[idx, d] = jnp.where(is_hi, hi, jnp.where(is_mid, mid, lo))


def _score_rhs(q_rows, aug):
    pad = jnp.zeros((CHAIN_W - HEAD_DIM - AUG_ROWS, ATT_BLOCK), BF16)
    return jnp.concatenate([q_rows, aug, pad], axis=0)


def _value_lhs(v_rows):
    return jnp.concatenate([v_rows, jnp.ones((AUG_ROWS, ATT_BLOCK), BF16)], axis=0)


def _attend_blocks(i, n_chains, score_fn, value_fn, s_refs, p_refs, alpha_refs, m_ref, acc_ref):
    chains = range(n_chains)
    kio = lax.broadcasted_iota(jnp.int32, (ATT_BLOCK, ATT_BLOCK), 0)
    qio = lax.broadcasted_iota(jnp.int32, (ATT_BLOCK, ATT_BLOCK), 1)

    def qk(n, slot):
        for c in chains:
            s_refs[slot][c] = score_fn(c, n)

    def softmax(slot, causal=False):
        for c in chains:
            s = s_refs[slot][c]
            if causal:
                s = jnp.where(kio <= qio, s, NEG_INF)
            m_run = m_ref[c]
            m_new = jnp.maximum(m_run, jnp.max(s, axis=0, keepdims=True))
            alpha_refs[slot][c] = jnp.exp2(m_run - m_new)
            m_ref[c] = m_new
            p_refs[slot][c] = jnp.exp2(s - m_new).astype(BF16)

    def pv(n, slot):
        for c in chains:
            acc_ref[c] = alpha_refs[slot][c] * acc_ref[c] + jnp.dot(value_fn(c, n), p_refs[slot][c],
                                                                    preferred_element_type=F32)

    for c in chains:
        m_ref[c] = jnp.full((1, ATT_BLOCK), NEG_INF, F32)
        acc_ref[c] = jnp.zeros(acc_ref.shape[1:], F32)
        p_refs[1][c] = jnp.zeros((ATT_BLOCK, ATT_BLOCK), BF16)
        alpha_refs[1][c] = jnp.ones((1, ATT_BLOCK), F32)
    qk(0, 0)

    def body(j, carry):
        n = 2 * j
        qk(n + 1, 1)
        softmax(0)
        pv(jnp.maximum(n - 1, 0), 1)

        @pl.when(n + 1 < i)
        def _():
            qk(n + 2, 0)
            softmax(1)
            pv(n, 0)

        return carry

    lax.fori_loop(0, (i + 1) // 2, body, 0)

    @pl.when(i % 2 == 0)
    def _():
        pv(jnp.maximum(i - 1, 0), 1)
        softmax(0, causal=True)
        pv(i, 0)

    @pl.when(i % 2 == 1)
    def _():
        pv(i - 1, 0)
        softmax(1, causal=True)
        pv(i, 1)


def _moba_kernel(slopes_ref, qT_ref, k_ref, vT_ref, o_ref, kmean_ref, kms_ref, sel_ref, base_ref, m_ref, acc_ref,
                 s0_ref, s1_ref, p0_ref, p1_ref, a0_ref, a1_ref, *, nblk):
    g = pl.program_id(1)
    n_chains = MOBA_CHAINS_PER_STEP
    chains = range(n_chains)

    def k_blk(c, n):
        return k_ref[0, n, :, c * CHAIN_W:(c + 1) * CHAIN_W]

    def v_lhs(c, n):
        return _value_lhs(vT_ref[0, n, c * HEAD_DIM:(c + 1) * HEAD_DIM, :])

    for c in chains:
        _build_bias_rows(base_ref, c, slopes_ref[g * n_chains + c] * LOG2E, nblk)
        for n in range(nblk):
            kmean_ref[n:n + 1, :] = jnp.mean(k_blk(c, n).astype(F32), axis=0, keepdims=True)
        hi, mid, lo = _split3(kmean_ref[...])
        kms_ref[c] = jnp.concatenate([hi, mid, lo], axis=0).astype(BF16)

    blk = lax.broadcasted_iota(jnp.int32, (nblk, ATT_BLOCK), 0)
    rowio = lax.broadcasted_iota(jnp.int32, (AUG_ROWS, ATT_BLOCK), 0)
    zero_rows = jnp.zeros((CHAIN_W - HEAD_DIM, ATT_BLOCK), BF16)

    def query_block(i, carry):
        def q_rows(c):
            return qT_ref[0, i, c * HEAD_DIM:(c + 1) * HEAD_DIM, :]

        for c in chains:
            x = jnp.dot(kms_ref[c], jnp.concatenate([q_rows(c), zero_rows], axis=0), preferred_element_type=F32)
            gate = x[0:nblk] + x[nblk:2 * nblk] + x[2 * nblk:3 * nblk]
            gate = jnp.where(blk < i, gate, NEG_INF)
            cnt = jnp.zeros((nblk, ATT_BLOCK), jnp.int32)
            for m in range(nblk):
                gm = gate[m:m + 1, :]
                cnt = cnt + jnp.where(gm > gate, 1, 0) + jnp.where(gm == gate, jnp.where(blk > m, 1, 0), 0)
            sel_ref[c] = jnp.where(blk < i, jnp.where(cnt < MOBA_TOPK, 1.0, 0.0), jnp.where(blk == i, 1.0, 0.0))

        def scores(c, n):
            base = base_ref[c, i - n]
            picked = sel_ref[c, pl.ds(n, 1), :] > 0.0
            aug = jnp.where(rowio == 0, jnp.where(picked, base, NEG_BIG), base).astype(BF16)
            return jnp.dot(k_blk(c, n), _score_rhs(q_rows(c), aug), preferred_element_type=F32)

        _attend_blocks(i, n_chains, scores, v_lhs, (s0_ref, s1_ref), (p0_ref, p1_ref), (a0_ref, a1_ref),
                       m_ref, acc_ref)
        rows = pl.ds(pl.multiple_of(i * ATT_BLOCK, ATT_BLOCK), ATT_BLOCK)
        for pair in range(n_chains // 2):
            outs = []
            for c in (2 * pair, 2 * pair + 1):
                a = acc_ref[c]
                outs.append(a[:HEAD_DIM] / a[HEAD_DIM:HEAD_DIM + 1])
            o_ref[0, rows, pair * 2 * HEAD_DIM:(pair + 1) * 2 * HEAD_DIM] = (
                jnp.concatenate(outs, axis=0).T.astype(BF16))
        return carry

    lax.fori_loop(0, nblk, query_block, 0)


def _moba(slopes, qT, k, vT):
    B, nblk = qT.shape[0], qT.shape[1]
    S = nblk * ATT_BLOCK
    nc = MOBA_CHAINS_PER_STEP
    width = nc * HEAD_DIM
    return pl.pallas_call(
        functools.partial(_moba_kernel, nblk=nblk),
        grid=(B, MOBA_HEADS // nc),
        in_specs=[
            pl.BlockSpec(memory_space=pltpu.SMEM),
            pl.BlockSpec((1, nblk, width, ATT_BLOCK), lambda b, g: (b, 0, g, 0)),
            pl.BlockSpec((1, nblk, ATT_BLOCK, nc * CHAIN_W), lambda b, g: (b, 0, 0, g)),
            pl.BlockSpec((1, nblk, width, ATT_BLOCK), lambda b, g: (b, 0, g, 0)),
        ],
        out_specs=pl.BlockSpec((1, S, width), lambda b, g: (b, 0, g)),
        out_shape=jax.ShapeDtypeStruct((B, S, MOBA_W), BF16),
        scratch_shapes=[
            pltpu.VMEM((nblk, CHAIN_W), F32),
            pltpu.VMEM((nc, 3 * nblk, CHAIN_W), BF16),
            pltpu.VMEM((nc, nblk, ATT_BLOCK), F32),
            pltpu.VMEM((nc, nblk, AUG_ROWS, ATT_BLOCK), F32),
            pltpu.VMEM((nc, 1, ATT_BLOCK), F32),
            pltpu.VMEM((nc, HEAD_DIM + AUG_ROWS, ATT_BLOCK), F32),
            pltpu.VMEM((nc, ATT_BLOCK, ATT_BLOCK), F32), pltpu.VMEM((nc, ATT_BLOCK, ATT_BLOCK), F32),
            pltpu.VMEM((nc, ATT_BLOCK, ATT_BLOCK), BF16), pltpu.VMEM((nc, ATT_BLOCK, ATT_BLOCK), BF16),
            pltpu.VMEM((nc, 1, ATT_BLOCK), F32), pltpu.VMEM((nc, 1, ATT_BLOCK), F32),
        ],
        compiler_params=pltpu.CompilerParams(dimension_semantics=("parallel", "parallel"),
                                             vmem_limit_bytes=VMEM_LIMIT),
        name="moba_attn",
    )(slopes, qT, k, vT)


def _diff_kernel(slopes_ref, lq1_ref, lk1_ref, lq2_ref, lk2_ref, g_ref, qT_ref, k_ref, vT_ref, o_ref,
                 base_ref, m_ref, acc_ref, s0_ref, s1_ref, p0_ref, p1_ref, a0_ref, a1_ref, *, lam_init, nblk):
    g = pl.program_id(1)
    n_heads = CHAINS_PER_STEP // 2
    chains = range(CHAINS_PER_STEP)
    lam = (jnp.exp(jnp.sum(lq1_ref[...] * lk1_ref[...], axis=-1, keepdims=True))
           - jnp.exp(jnp.sum(lq2_ref[...] * lk2_ref[...], axis=-1, keepdims=True)) + lam_init)

    def k_blk(c, n):
        return k_ref[0, n, :, c * CHAIN_W:(c + 1) * CHAIN_W]

    def v_lhs(c, n):
        hd = c // 2
        return _value_lhs(vT_ref[0, n, hd * DIFF_V_DIM:(hd + 1) * DIFF_V_DIM, :])

    for hd in range(n_heads):
        _build_bias_rows(base_ref, hd, slopes_ref[g * n_heads + hd] * LOG2E, nblk)

    def query_block(i, carry):
        def scores(c, n):
            q_rows = qT_ref[0, i, c * HEAD_DIM:(c + 1) * HEAD_DIM, :]
            return jnp.dot(k_blk(c, n), _score_rhs(q_rows, base_ref[c // 2, i - n].astype(BF16)),
                           preferred_element_type=F32)

        _attend_blocks(i, CHAINS_PER_STEP, scores, v_lhs, (s0_ref, s1_ref), (p0_ref, p1_ref), (a0_ref, a1_ref),
                       m_ref, acc_ref)
        rows = pl.ds(pl.multiple_of(i * ATT_BLOCK, ATT_BLOCK), ATT_BLOCK)
        for hd in range(n_heads):
            a1 = acc_ref[2 * hd]
            a2 = acc_ref[2 * hd + 1]
            o = (a1[:DIFF_V_DIM] / a1[DIFF_V_DIM:DIFF_V_DIM + 1]
                 - lam * (a2[:DIFF_V_DIM] / a2[DIFF_V_DIM:DIFF_V_DIM + 1]))
            o = o * lax.rsqrt(jnp.mean(o * o, axis=0, keepdims=True) + RMS_EPS)
            o = o * g_ref[...] * (1.0 - lam_init)
            o_ref[0, rows, hd * DIFF_V_DIM:(hd + 1) * DIFF_V_DIM] = o.T.astype(BF16)
        return carry

    lax.fori_loop(0, nblk, query_block, 0)


def _diff(slopes, lq1, lk1, lq2, lk2, g_col, qT, k, vT, lam_init):
    B, nblk = qT.shape[0], qT.shape[1]
    S = nblk * ATT_BLOCK
    nc = CHAINS_PER_STEP
    n_heads = nc // 2
    width = n_heads * DIFF_V_DIM
    off = MOBA_W // width
    vec = pl.BlockSpec((1, HEAD_DIM), lambda b, g: (0, 0))
    return pl.pallas_call(
        functools.partial(_diff_kernel, lam_init=lam_init, nblk=nblk),
        grid=(B, DIFF_HEADS // n_heads),
        in_specs=[
            pl.BlockSpec(memory_space=pltpu.SMEM),
            vec, vec, vec, vec,
            pl.BlockSpec((DIFF_V_DIM, 1), lambda b, g: (0, 0)),
            pl.BlockSpec((1, nblk, width, ATT_BLOCK), lambda b, g: (b, 0, off + g, 0)),
            pl.BlockSpec((1, nblk, ATT_BLOCK, nc * CHAIN_W), lambda b, g: (b, 0, 0, off + g)),
            pl.BlockSpec((1, nblk, width, ATT_BLOCK), lambda b, g: (b, 0, off + g, 0)),
        ],
        out_specs=pl.BlockSpec((1, S, width), lambda b, g: (b, 0, g)),
        out_shape=jax.ShapeDtypeStruct((B, S, DIFF_HEADS * DIFF_V_DIM), BF16),
        scratch_shapes=[
            pltpu.VMEM((n_heads, nblk, AUG_ROWS, ATT_BLOCK), F32),
            pltpu.VMEM((nc, 1, ATT_BLOCK), F32),
            pltpu.VMEM((nc, DIFF_V_DIM + AUG_ROWS, ATT_BLOCK), F32),
            pltpu.VMEM((nc, ATT_BLOCK, ATT_BLOCK), F32), pltpu.VMEM((nc, ATT_BLOCK, ATT_BLOCK), F32),
            pltpu.VMEM((nc, ATT_BLOCK, ATT_BLOCK), BF16), pltpu.VMEM((nc, ATT_BLOCK, ATT_BLOCK), BF16),
            pltpu.VMEM((nc, 1, ATT_BLOCK), F32), pltpu.VMEM((nc, 1, ATT_BLOCK), F32),
        ],
        compiler_params=pltpu.CompilerParams(dimension_semantics=("parallel", "parallel"),
                                             vmem_limit_bytes=VMEM_LIMIT),
        name="diff_attn",
    )(slopes, lq1, lk1, lq2, lk2, g_col, qT, k, vT)


def _alibi_slopes(n):
    return jnp.asarray(2.0 ** (-8.0 * np.arange(1, n + 1) / n), dtype=F32)


def _ffn_weights(w_gate, w_up, w_down):
    chunk = lambda w: w.reshape(D_MODEL, N_FF_CHUNKS, FF_CHUNK).transpose(1, 0, 2)
    wgu = jnp.concatenate([chunk(w_gate), chunk(w_up)], axis=-1).astype(BF16)
    wd = w_down.reshape(N_FF_CHUNKS, FF_CHUNK, D_MODEL).astype(BF16)
    return wgu, wd


def _k_bias_pattern():
    r = np.arange(ATT_BLOCK)
    cols = np.zeros((ATT_BLOCK, CHAIN_W), np.float32)
    cols[:, HEAD_DIM:HEAD_DIM + 3] = 1.0
    cols[:, HEAD_DIM + 3:HEAD_DIM + 6] = (r // 16)[:, None]
    cols[:, HEAD_DIM + 6:HEAD_DIM + 9] = (r % 16)[:, None]
    return jnp.asarray(np.tile(cols, (1, N_CHAINS)))


def kernel(x, ffn1_pre_g, ffn1_w_gate, ffn1_w_up, ffn1_w_down, ffn1_post_g, mix_pre_g, w_in, lambda_q1, lambda_k1, lambda_q2, lambda_k2, subln_g, w_out, mix_post_g, ffn2_pre_g, ffn2_w_gate, ffn2_w_up, ffn2_w_down, ffn2_post_g):
    B, S, D = x.shape
    depth = w_in.shape[0]
    slopes_m = _alibi_slopes(MOBA_HEADS)
    slopes_d = _alibi_slopes(DIFF_HEADS)
    kpat = _k_bias_pattern()
    row = lambda v: v.reshape(1, -1)
    x2 = x.reshape(B * S, D)
    for l in range(depth):
        wgu, wd = _ffn_weights(ffn1_w_gate[l], ffn1_w_up[l], ffn1_w_down[l])
        x2 = _ffn(x2, row(ffn1_pre_g[l]), wgu, wd, row(ffn1_post_g[l]))

        w = w_in[l]
        wq = jnp.concatenate([w[:, 0:512], w[:, 1536:2048]], axis=1)
        wk = jnp.concatenate([w[:, 512:1024], w[:, 2048:2560]], axis=1).reshape(D, N_CHAINS, HEAD_DIM)
        wk = jnp.pad(wk, ((0, 0), (0, 0), (0, CHAIN_W - HEAD_DIM))).reshape(D, N_CHAINS * CHAIN_W)
        wv = jnp.concatenate([w[:, 1024:1536], w[:, 2560:3072]], axis=1)
        qT, k, vT = _proj(x2.reshape(B, S, D), row(mix_pre_g[l]), wq.T.astype(BF16), wk.astype(BF16),
                          wv.T.astype(BF16), kpat)
        o_m = _moba(slopes_m, qT, k, vT)
        lam_init = 0.8 - 0.6 * math.exp(-0.3 * l)
        o_d = _diff(slopes_d, row(lambda_q1[l]), row(lambda_k1[l]), row(lambda_q2[l]), row(lambda_k2[l]),
                    subln_g[l].reshape(DIFF_V_DIM, 1), qT, k, vT, lam_init)
        wo = w_out[l].astype(BF16)
        wgu, wd = _ffn_weights(ffn2_w_gate[l], ffn2_w_up[l], ffn2_w_down[l])
        x2 = _mix_ffn(x2, o_m.reshape(B * S, MOBA_W), o_d.reshape(B * S, MOBA_W), wo[:MOBA_W], wo[MOBA_W:],
                      row(mix_post_g[l]), row(ffn2_pre_g[l]), wgu, wd, row(ffn2_post_g[l]))
    return x2.reshape(B, S, D)
```

```python
import functools
import math

import jax
import jax.numpy as jnp
import numpy as np
from jax import lax
from jax.experimental import pallas as pl
from jax.experimental.pallas import tpu as pltpu

F32 = jnp.float32
BF16 = jnp.bfloat16

D_MODEL = 1024
HEAD_DIM = 64
MOBA_HEADS = 8
MOBA_BLOCK = 256
MOBA_TOPK = 3
DIFF_HEADS = 4
DIFF_V_DIM = 128
MOBA_W = 512
D_FF = 2816
RMS_EPS = 1e-6

FF_CHUNK = 256
N_FF_CHUNKS = D_FF // FF_CHUNK
TOKEN_TILE = 512
ATT_BLOCK = 256
CHAIN_W = 128
N_CHAINS = MOBA_HEADS + 2 * DIFF_HEADS
AUG_ROWS = 16
CHAINS_PER_STEP = 8
MOBA_CHAINS_PER_STEP = 8
VMEM_LIMIT = 56 * 1024 * 1024

LOG2E = 1.4426950408889634
Q_SCALE = HEAD_DIM ** -0.5 * LOG2E
NEG_INF = float("-inf")
NEG_BIG = -1e30


def _rms(x):
    return x * lax.rsqrt(jnp.mean(x * x, axis=-1, keepdims=True) + RMS_EPS)


def _ffn_kernel(x_ref, gpre_ref, wgu_ref, wd_ref, gpost_ref, o_ref, h_ref, a_ref, acc_ref):
    x = x_ref[...]
    h_ref[...] = (_rms(x) * gpre_ref[...]).astype(BF16)
    acc_ref[...] = jnp.zeros_like(acc_ref)

    def hidden(c):
        gu = jnp.dot(h_ref[...], wgu_ref[c], preferred_element_type=F32)
        g = gu[:, :FF_CHUNK]
        u = gu[:, FF_CHUNK:]
        return (g * jax.nn.sigmoid(g) * u).astype(BF16)

    def project(slot, c):
        acc_ref[...] += jnp.dot(a_ref[slot], wd_ref[c], preferred_element_type=F32)

    a_ref[0] = hidden(0)

    def body(j, carry):
        c = 2 * j
        nxt = hidden(c + 1)
        project(0, c)
        a_ref[1] = nxt
        nxt = hidden(c + 2)
        project(1, c + 1)
        a_ref[0] = nxt
        return carry

    assert N_FF_CHUNKS % 2 == 1
    lax.fori_loop(0, N_FF_CHUNKS // 2, body, 0)
    y = acc_ref[...] + jnp.dot(a_ref[0], wd_ref[N_FF_CHUNKS - 1], preferred_element_type=F32)
    o_ref[...] = x_ref[...] + 0.5 * (_rms(y) * gpost_ref[...])


def _ffn(x2, gpre, wgu, wd, gpost):
    T = x2.shape[0]
    const = lambda *shape: pl.BlockSpec(shape, lambda i: (0,) * len(shape), pipeline_mode=pl.Buffered(1))
    return pl.pallas_call(
        _ffn_kernel,
        grid=(T // TOKEN_TILE,),
        in_specs=[
            pl.BlockSpec((TOKEN_TILE, D_MODEL), lambda i: (i, 0)),
            const(1, D_MODEL),
            const(N_FF_CHUNKS, D_MODEL, 2 * FF_CHUNK),
            const(N_FF_CHUNKS, FF_CHUNK, D_MODEL),
            const(1, D_MODEL),
        ],
        out_specs=pl.BlockSpec((TOKEN_TILE, D_MODEL), lambda i: (i, 0)),
        out_shape=jax.ShapeDtypeStruct((T, D_MODEL), F32),
        scratch_shapes=[pltpu.VMEM((TOKEN_TILE, D_MODEL), BF16), pltpu.VMEM((2, TOKEN_TILE, FF_CHUNK), BF16),
                        pltpu.VMEM((TOKEN_TILE, D_MODEL), F32)],
        compiler_params=pltpu.CompilerParams(dimension_semantics=("parallel",), vmem_limit_bytes=VMEM_LIMIT),
        name="ffn",
    )(x2, gpre, wgu, wd, gpost)


def _mix_ffn_kernel(x_ref, om_ref, od_ref, wom_ref, wod_ref, gmix_ref, gpre_ref, wgu_ref, wd_ref, gpost_ref,
                    o_ref, h_ref, a_ref, acc_ref):
    y = (jnp.dot(om_ref[...], wom_ref[...], preferred_element_type=F32)
         + jnp.dot(od_ref[...], wod_ref[...], preferred_element_type=F32))
    o_ref[...] = x_ref[...] + _rms(y) * gmix_ref[...]
    _ffn_kernel(o_ref, gpre_ref, wgu_ref, wd_ref, gpost_ref, o_ref, h_ref, a_ref, acc_ref)


def _mix_ffn(x2, om, od, wom, wod, gmix, gpre, wgu, wd, gpost):
    T = x2.shape[0]
    const = lambda *shape: pl.BlockSpec(shape, lambda i: (0,) * len(shape), pipeline_mode=pl.Buffered(1))
    tile = lambda w: pl.BlockSpec((TOKEN_TILE, w), lambda i: (i, 0))
    return pl.pallas_call(
        _mix_ffn_kernel,
        grid=(T // TOKEN_TILE,),
        in_specs=[
            tile(D_MODEL), tile(MOBA_W), tile(MOBA_W),
            const(MOBA_W, D_MODEL), const(MOBA_W, D_MODEL), const(1, D_MODEL),
            const(1, D_MODEL),
            const(N_FF_CHUNKS, D_MODEL, 2 * FF_CHUNK),
            const(N_FF_CHUNKS, FF_CHUNK, D_MODEL),
            const(1, D_MODEL),
        ],
        out_specs=tile(D_MODEL),
        out_shape=jax.ShapeDtypeStruct((T, D_MODEL), F32),
        scratch_shapes=[pltpu.VMEM((TOKEN_TILE, D_MODEL), BF16), pltpu.VMEM((2, TOKEN_TILE, FF_CHUNK), BF16),
                        pltpu.VMEM((TOKEN_TILE, D_MODEL), F32)],
        compiler_params=pltpu.CompilerParams(dimension_semantics=("parallel",), vmem_limit_bytes=VMEM_LIMIT),
        name="mix_out_ffn",
    )(x2, om, od, wom, wod, gmix, gpre, wgu, wd, gpost)


def _proj_kernel(x_ref, g_ref, wqT_ref, wk_ref, wvT_ref, kpat_ref, qT_ref, k_ref, vT_ref):
    h = (_rms(x_ref[0]) * g_ref[...]).astype(BF16)
    nt = (((1,), (1,)), ((), ()))
    kk = jnp.dot(h, wk_ref[...], preferred_element_type=F32)
    qT = (lax.dot_general(wqT_ref[...], h, nt, preferred_element_type=F32) * Q_SCALE).astype(BF16)
    vT = lax.dot_general(wvT_ref[...], h, nt, preferred_element_type=F32).astype(BF16)
    for j in range(TOKEN_TILE // ATT_BLOCK):
        sl = slice(j * ATT_BLOCK, (j + 1) * ATT_BLOCK)
        k_ref[0, j] = (kk[sl, :] + kpat_ref[...]).astype(BF16)
        qT_ref[0, j] = qT[:, sl]
        vT_ref[0, j] = vT[:, sl]


def _proj(x, g, wqT, wk, wvT, kpat):
    B, S, _ = x.shape
    nblk = S // ATT_BLOCK
    per = TOKEN_TILE // ATT_BLOCK
    k_w = N_CHAINS * CHAIN_W
    const = lambda *shape: pl.BlockSpec(shape, lambda b, i: (0,) * len(shape), pipeline_mode=pl.Buffered(1))
    t_spec = pl.BlockSpec((1, per, D_MODEL, ATT_BLOCK), lambda b, i: (b, i, 0, 0))
    t_shape = jax.ShapeDtypeStruct((B, nblk, D_MODEL, ATT_BLOCK), BF16)
    return pl.pallas_call(
        _proj_kernel,
        grid=(B, S // TOKEN_TILE),
        in_specs=[
            pl.BlockSpec((1, TOKEN_TILE, D_MODEL), lambda b, i: (b, i, 0)),
            const(1, D_MODEL),
            const(D_MODEL, D_MODEL),
            const(D_MODEL, k_w),
            const(D_MODEL, D_MODEL),
            const(ATT_BLOCK, k_w),
        ],
        out_specs=[t_spec, pl.BlockSpec((1, per, ATT_BLOCK, k_w), lambda b, i: (b, i, 0, 0)), t_spec],
        out_shape=[t_shape, jax.ShapeDtypeStruct((B, nblk, ATT_BLOCK, k_w), BF16), t_shape],
        compiler_params=pltpu.CompilerParams(dimension_semantics=("parallel", "parallel"),
                                             vmem_limit_bytes=VMEM_LIMIT),
        name="mix_proj",
    )(x, g, wqT, wk, wvT, kpat)


def _split3(t):
    hi = t.astype(BF16).astype(F32)
    r = t - hi
    mid = r.astype(BF16).astype(F32)
    lo = (r - mid).astype(BF16).astype(F32)
    return hi, mid, lo


def _build_bias_rows(base_ref, idx, slope2, nblk):
    rowio = lax.broadcasted_iota(jnp.int32, (AUG_ROWS, ATT_BLOCK), 0)
    is_hi = (rowio == 0) | (rowio == 3) | (rowio == 6)
    is_mid = (rowio == 1) | (rowio == 4) | (rowio == 7)
    for d in range(nblk):
        off = slope2 * (-float(ATT_BLOCK * d))
        t = jnp.where(rowio < 3, off, jnp.where(rowio < 6, 16.0 * slope2, jnp.where(rowio < 9, slope2, 0.0)))
        hi, mid, lo = _split3(t)
        base_ref[idx, d] = jnp.where(is_hi, hi, jnp.where(is_mid, mid, lo))


def _score_rhs(q_rows, aug):
    pad = jnp.zeros((CHAIN_W - HEAD_DIM - AUG_ROWS, ATT_BLOCK), BF16)
    return jnp.concatenate([q_rows, aug, pad], axis=0)


def _value_lhs(v_rows):
    return jnp.concatenate([v_rows, jnp.ones((AUG_ROWS, ATT_BLOCK), BF16)], axis=0)


def _attend_blocks(i, n_chains, score_fn, value_fn, s_refs, p_refs, alpha_refs, m_ref, acc_ref):
    chains = range(n_chains)
    kio = lax.broadcasted_iota(jnp.int32, (ATT_BLOCK, ATT_BLOCK), 0)
    qio = lax.broadcasted_iota(jnp.int32, (ATT_BLOCK, ATT_BLOCK), 1)

    def qk(n, slot):
        for c in chains:
            s_refs[slot][c] = score_fn(c, n)

    def softmax(slot, causal=False):
        for c in chains:
            s = s_refs[slot][c]
            if causal:
                s = jnp.where(kio <= qio, s, NEG_INF)
            m_run = m_ref[c]
            m_new = jnp.maximum(m_run, jnp.max(s, axis=0, keepdims=True))
            alpha_refs[slot][c] = jnp.exp2(m_run - m_new)
            m_ref[c] = m_new
            p_refs[slot][c] = jnp.exp2(s - m_new).astype(BF16)

    def pv(n, slot):
        for c in chains:
            acc_ref[c] = alpha_refs[slot][c] * acc_ref[c] + jnp.dot(value_fn(c, n), p_refs[slot][c],
                                                                    preferred_element_type=F32)

    for c in chains:
        m_ref[c] = jnp.full((1, ATT_BLOCK), NEG_INF, F32)
        acc_ref[c] = jnp.zeros(acc_ref.shape[1:], F32)
        p_refs[1][c] = jnp.zeros((ATT_BLOCK, ATT_BLOCK), BF16)
        alpha_refs[1][c] = jnp.ones((1, ATT_BLOCK), F32)
    qk(0, 0)

    def body(j, carry):
        n = 2 * j
        qk(n + 1, 1)
        softmax(0)
        pv(jnp.maximum(n - 1, 0), 1)

        @pl.when(n + 1 < i)
        def _():
            qk(n + 2, 0)
            softmax(1)
            pv(n, 0)

        return carry

    lax.fori_loop(0, (i + 1) // 2, body, 0)

    @pl.when(i % 2 == 0)
    def _():
        pv(jnp.maximum(i - 1, 0), 1)
        softmax(0, causal=True)
        pv(i, 0)

    @pl.when(i % 2 == 1)
    def _():
        pv(i - 1, 0)
        softmax(1, causal=True)
        pv(i, 1)


def _moba_kernel(slopes_ref, qT_ref, k_ref, vT_ref, o_ref, kmean_ref, kms_ref, sel_ref, base_ref, m_ref, acc_ref,
                 s0_ref, s1_ref, p0_ref, p1_ref, a0_ref, a1_ref, *, nblk):
    g = pl.program_id(1)
    n_chains = MOBA_CHAINS_PER_STEP
    chains = range(n_chains)

    def k_blk(c, n):
        return k_ref[0, n, :, c * CHAIN_W:(c + 1) * CHAIN_W]

    def v_lhs(c, n):
        return _value_lhs(vT_ref[0, n, c * HEAD_DIM:(c + 1) * HEAD_DIM, :])

    for c in chains:
        _build_bias_rows(base_ref, c, slopes_ref[g * n_chains + c] * LOG2E, nblk)
        for n in range(nblk):
            kmean_ref[n:n + 1, :] = jnp.mean(k_blk(c, n).astype(F32), axis=0, keepdims=True)
        hi, mid, lo = _split3(kmean_ref[...])
        kms_ref[c] = jnp.concatenate([hi, mid, lo], axis=0).astype(BF16)

    blk = lax.broadcasted_iota(jnp.int32, (nblk, ATT_BLOCK), 0)
    rowio = lax.broadcasted_iota(jnp.int32, (AUG_ROWS, ATT_BLOCK), 0)
    zero_rows = jnp.zeros((CHAIN_W - HEAD_DIM, ATT_BLOCK), BF16)

    def query_block(i, carry):
        def q_rows(c):
            return qT_ref[0, i, c * HEAD_DIM:(c + 1) * HEAD_DIM, :]

        for c in chains:
            x = jnp.dot(kms_ref[c], jnp.concatenate([q_rows(c), zero_rows], axis=0), preferred_element_type=F32)
            gate = x[0:nblk] + x[nblk:2 * nblk] + x[2 * nblk:3 * nblk]
            gate = jnp.where(blk < i, gate, NEG_INF)
            cnt = jnp.zeros((nblk, ATT_BLOCK), jnp.int32)
            for m in range(nblk):
                gm = gate[m:m + 1, :]
                cnt = cnt + jnp.where(gm > gate, 1, 0) + jnp.where(gm == gate, jnp.where(blk > m, 1, 0), 0)
            sel_ref[c] = jnp.where(blk < i, jnp.where(cnt < MOBA_TOPK, 1.0, 0.0), jnp.where(blk == i, 1.0, 0.0))

        def scores(c, n):
            base = base_ref[c, i - n]
            picked = sel_ref[c, pl.ds(n, 1), :] > 0.0
            aug = jnp.where(rowio == 0, jnp.where(picked, base, NEG_BIG), base).astype(BF16)
            return jnp.dot(k_blk(c, n), _score_rhs(q_rows(c), aug), preferred_element_type=F32)

        _attend_blocks(i, n_chains, scores, v_lhs, (s0_ref, s1_ref), (p0_ref, p1_ref), (a0_ref, a1_ref),
                       m_ref, acc_ref)
        rows = pl.ds(pl.multiple_of(i * ATT_BLOCK, ATT_BLOCK), ATT_BLOCK)
        for pair in range(n_chains // 2):
            outs = []
            for c in (2 * pair, 2 * pair + 1):
                a = acc_ref[c]
                outs.append(a[:HEAD_DIM] / a[HEAD_DIM:HEAD_DIM + 1])
            o_ref[0, rows, pair * 2 * HEAD_DIM:(pair + 1) * 2 * HEAD_DIM] = (
                jnp.concatenate(outs, axis=0).T.astype(BF16))
        return carry

    lax.fori_loop(0, nblk, query_block, 0)


def _moba(slopes, qT, k, vT):
    B, nblk = qT.shape[0], qT.shape[1]
    S = nblk * ATT_BLOCK
    nc = MOBA_CHAINS_PER_STEP
    width = nc * HEAD_DIM
    return pl.pallas_call(
        functools.partial(_moba_kernel, nblk=nblk),
        grid=(B, MOBA_HEADS // nc),
        in_specs=[
            pl.BlockSpec(memory_space=pltpu.SMEM),
            pl.BlockSpec((1, nblk, width, ATT_BLOCK), lambda b, g: (b, 0, g, 0)),
            pl.BlockSpec((1, nblk, ATT_BLOCK, nc * CHAIN_W), lambda b, g: (b, 0, 0, g)),
            pl.BlockSpec((1, nblk, width, ATT_BLOCK), lambda b, g: (b, 0, g, 0)),
        ],
        out_specs=pl.BlockSpec((1, S, width), lambda b, g: (b, 0, g)),
        out_shape=jax.ShapeDtypeStruct((B, S, MOBA_W), BF16),
        scratch_shapes=[
            pltpu.VMEM((nblk, CHAIN_W), F32),
            pltpu.VMEM((nc, 3 * nblk, CHAIN_W), BF16),
            pltpu.VMEM((nc, nblk, ATT_BLOCK), F32),
            pltpu.VMEM((nc, nblk, AUG_ROWS, ATT_BLOCK), F32),
            pltpu.VMEM((nc, 1, ATT_BLOCK), F32),
            pltpu.VMEM((nc, HEAD_DIM + AUG_ROWS, ATT_BLOCK), F32),
            pltpu.VMEM((nc, ATT_BLOCK, ATT_BLOCK), F32), pltpu.VMEM((nc, ATT_BLOCK, ATT_BLOCK), F32),
            pltpu.VMEM((nc, ATT_BLOCK, ATT_BLOCK), BF16), pltpu.VMEM((nc, ATT_BLOCK, ATT_BLOCK), BF16),
            pltpu.VMEM((nc, 1, ATT_BLOCK), F32), pltpu.VMEM((nc, 1, ATT_BLOCK), F32),
        ],
        compiler_params=pltpu.CompilerParams(dimension_semantics=("parallel", "parallel"),
                                             vmem_limit_bytes=VMEM_LIMIT),
        name="moba_attn",
    )(slopes, qT, k, vT)


def _diff_kernel(slopes_ref, lq1_ref, lk1_ref, lq2_ref, lk2_ref, g_ref, qT_ref, k_ref, vT_ref, o_ref,
                 base_ref, m_ref, acc_ref, s0_ref, s1_ref, p0_ref, p1_ref, a0_ref, a1_ref, *, lam_init, nblk):
    g = pl.program_id(1)
    n_heads = CHAINS_PER_STEP // 2
    chains = range(CHAINS_PER_STEP)
    lam = (jnp.exp(jnp.sum(lq1_ref[...] * lk1_ref[...], axis=-1, keepdims=True))
           - jnp.exp(jnp.sum(lq2_ref[...] * lk2_ref[...], axis=-1, keepdims=True)) + lam_init)

    def k_blk(c, n):
        return k_ref[0, n, :, c * CHAIN_W:(c + 1) * CHAIN_W]

    def v_lhs(c, n):
        hd = c // 2
        return _value_lhs(vT_ref[0, n, hd * DIFF_V_DIM:(hd + 1) * DIFF_V_DIM, :])

    for hd in range(n_heads):
        _build_bias_rows(base_ref, hd, slopes_ref[g * n_heads + hd] * LOG2E, nblk)

    def query_block(i, carry):
        def scores(c, n):
            q_rows = qT_ref[0, i, c * HEAD_DIM:(c + 1) * HEAD_DIM, :]
            return jnp.dot(k_blk(c, n), _score_rhs(q_rows, base_ref[c // 2, i - n].astype(BF16)),
                           preferred_element_type=F32)

        _attend_blocks(i, CHAINS_PER_STEP, scores, v_lhs, (s0_ref, s1_ref), (p0_ref, p1_ref), (a0_ref, a1_ref),
                       m_ref, acc_ref)
        rows = pl.ds(pl.multiple_of(i * ATT_BLOCK, ATT_BLOCK), ATT_BLOCK)
        for hd in range(n_heads):
            a1 = acc_ref[2 * hd]
            a2 = acc_ref[2 * hd + 1]
            o = (a1[:DIFF_V_DIM] / a1[DIFF_V_DIM:DIFF_V_DIM + 1]
                 - lam * (a2[:DIFF_V_DIM] / a2[DIFF_V_DIM:DIFF_V_DIM + 1]))
            o = o * lax.rsqrt(jnp.mean(o * o, axis=0, keepdims=True) + RMS_EPS)
            o = o * g_ref[...] * (1.0 - lam_init)
            o_ref[0, rows, hd * DIFF_V_DIM:(hd + 1) * DIFF_V_DIM] = o.T.astype(BF16)
        return carry

    lax.fori_loop(0, nblk, query_block, 0)


def _diff(slopes, lq1, lk1, lq2, lk2, g_col, qT, k, vT, lam_init):
    B, nblk = qT.shape[0], qT.shape[1]
    S = nblk * ATT_BLOCK
    nc = CHAINS_PER_STEP
    n_heads = nc // 2
    width = n_heads * DIFF_V_DIM
    off = MOBA_W // width
    vec = pl.BlockSpec((1, HEAD_DIM), lambda b, g: (0, 0))
    return pl.pallas_call(
        functools.partial(_diff_kernel, lam_init=lam_init, nblk=nblk),
        grid=(B, DIFF_HEADS // n_heads),
        in_specs=[
            pl.BlockSpec(memory_space=pltpu.SMEM),
            vec, vec, vec, vec,
            pl.BlockSpec((DIFF_V_DIM, 1), lambda b, g: (0, 0)),
            pl.BlockSpec((1, nblk, width, ATT_BLOCK), lambda b, g: (b, 0, off + g, 0)),
            pl.BlockSpec((1, nblk, ATT_BLOCK, nc * CHAIN_W), lambda b, g: (b, 0, 0, off + g)),
            pl.BlockSpec((1, nblk, width, ATT_BLOCK), lambda b, g: (b, 0, off + g, 0)),
        ],
        out_specs=pl.BlockSpec((1, S, width), lambda b, g: (b, 0, g)),
        out_shape=jax.ShapeDtypeStruct((B, S, DIFF_HEADS * DIFF_V_DIM), BF16),
        scratch_shapes=[
            pltpu.VMEM((n_heads, nblk, AUG_ROWS, ATT_BLOCK), F32),
            pltpu.VMEM((nc, 1, ATT_BLOCK), F32),
            pltpu.VMEM((nc, DIFF_V_DIM + AUG_ROWS, ATT_BLOCK), F32),
            pltpu.VMEM((nc, ATT_BLOCK, ATT_BLOCK), F32), pltpu.VMEM((nc, ATT_BLOCK, ATT_BLOCK), F32),
            pltpu.VMEM((nc, ATT_BLOCK, ATT_BLOCK), BF16), pltpu.VMEM((nc, ATT_BLOCK, ATT_BLOCK), BF16),
            pltpu.VMEM((nc, 1, ATT_BLOCK), F32), pltpu.VMEM((nc, 1, ATT_BLOCK), F32),
        ],
        compiler_params=pltpu.CompilerParams(dimension_semantics=("parallel", "parallel"),
                                             vmem_limit_bytes=VMEM_LIMIT),
        name="diff_attn",
    )(slopes, lq1, lk1, lq2, lk2, g_col, qT, k, vT)


def _alibi_slopes(n):
    return jnp.asarray(2.0 ** (-8.0 * np.arange(1, n + 1) / n), dtype=F32)


def _ffn_weights(w_gate, w_up, w_down):
    chunk = lambda w: w.reshape(D_MODEL, N_FF_CHUNKS, FF_CHUNK).transpose(1, 0, 2)
    wgu = jnp.concatenate([chunk(w_gate), chunk(w_up)], axis=-1).astype(BF16)
    wd = w_down.reshape(N_FF_CHUNKS, FF_CHUNK, D_MODEL).astype(BF16)
    return wgu, wd


def _k_bias_pattern():
    r = np.arange(ATT_BLOCK)
    cols = np.zeros((ATT_BLOCK, CHAIN_W), np.float32)
    cols[:, HEAD_DIM:HEAD_DIM + 3] = 1.0
    cols[:, HEAD_DIM + 3:HEAD_DIM + 6] = (r // 16)[:, None]
    cols[:, HEAD_DIM + 6:HEAD_DIM + 9] = (r % 16)[:, None]
    return jnp.asarray(np.tile(cols, (1, N_CHAINS)))


def kernel(x, ffn1_pre_g, ffn1_w_gate, ffn1_w_up, ffn1_w_down, ffn1_post_g, mix_pre_g, w_in, lambda_q1, lambda_k1, lambda_q2, lambda_k2, subln_g, w_out, mix_post_g, ffn2_pre_g, ffn2_w_gate, ffn2_w_up, ffn2_w_down, ffn2_post_g):
    B, S, D = x.shape
    depth = w_in.shape[0]
    slopes_m = _alibi_slopes(MOBA_HEADS)
    slopes_d = _alibi_slopes(DIFF_HEADS)
    kpat = _k_bias_pattern()
    row = lambda v: v.reshape(1, -1)
    x2 = x.reshape(B * S, D)
    for l in range(depth):
        wgu, wd = _ffn_weights(ffn1_w_gate[l], ffn1_w_up[l], ffn1_w_down[l])
        x2 = _ffn(x2, row(ffn1_pre_g[l]), wgu, wd, row(ffn1_post_g[l]))

        w = w_in[l]
        wq = jnp.concatenate([w[:, 0:512], w[:, 1536:2048]], axis=1)
        wk = jnp.concatenate([w[:, 512:1024], w[:, 2048:2560]], axis=1).reshape(D, N_CHAINS, HEAD_DIM)
        wk = jnp.pad(wk, ((0, 0), (0, 0), (0, CHAIN_W - HEAD_DIM))).reshape(D, N_CHAINS * CHAIN_W)
        wv = jnp.concatenate([w[:, 1024:1536], w[:, 2560:3072]], axis=1)
        qT, k, vT = _proj(x2.reshape(B, S, D), row(mix_pre_g[l]), wq.T.astype(BF16), wk.astype(BF16),
                          wv.T.astype(BF16), kpat)
        o_m = _moba(slopes_m, qT, k, vT)
        lam_init = 0.8 - 0.6 * math.exp(-0.3 * l)
        o_d = _diff(slopes_d, row(lambda_q1[l]), row(lambda_k1[l]), row(lambda_q2[l]), row(lambda_k2[l]),
                    subln_g[l].reshape(DIFF_V_DIM, 1), qT, k, vT, lam_init)
        wo = w_out[l].astype(BF16)
        wgu, wd = _ffn_weights(ffn2_w_gate[l], ffn2_w_up[l], ffn2_w_down[l])
        x2 = _mix_ffn(x2, o_m.reshape(B * S, MOBA_W), o_d.reshape(B * S, MOBA_W), wo[:MOBA_W], wo[MOBA_W:],
                      row(mix_post_g[l]), row(ffn2_pre_g[l]), wgu, wd, row(ffn2_post_g[l]))
    return x2.reshape(B, S, D)
```
